```python
import math
import jax, jax.numpy as jnp
from jax import lax
import numpy as np

D_MODEL = 1024
BATCH = 8
SEQ = 8192
DEPTH = 1

N_META = 16
ATT_HEADS = 8
ATT_KV_HEADS = 2
ATT_GROUP = ATT_HEADS // ATT_KV_HEADS
HEAD_DIM = 64
WINDOW = 128
BLOCK = 128
RWKV_HEADS = 8
RWKV_HEAD_DIM = 64
RWKV_DIM = RWKV_HEADS * RWKV_HEAD_DIM
DECAY_LORA = 64
AAA_LORA = 64
GATE_LORA = 128
LN_X_EPS = 64e-5
N_BRANCH = 2
ATT_Q = ATT_HEADS * HEAD_DIM
ATT_KV = ATT_KV_HEADS * HEAD_DIM
ATT_COLS = ATT_Q + 2 * ATT_KV
RWKV_COLS = 3 * RWKV_DIM + DECAY_LORA + AAA_LORA + GATE_LORA
GATE_COLS = N_BRANCH * D_MODEL
IN_COLS = ATT_COLS + RWKV_COLS + GATE_COLS
N_GROUPS = 4
EXPERTS_PER_GROUP = 8
N_EXPERTS = N_GROUPS * EXPERTS_PER_GROUP
TOP_K = 2
D_EXPERT = 256
MOE_BLOCK = 128
NORM_EPS = 1e-6

kernel_name = "hybrid_swa_rwkv7_hiermoe_block"


def rmsnorm(x, g):
    x32 = x.astype(jnp.float32)
    y = x32 * lax.rsqrt(jnp.mean(x32 * x32, axis=-1, keepdims=True) + NORM_EPS)
    return y.astype(x.dtype) * g


def alibi_slopes(n_heads):
    return jnp.asarray(np.array([2.0 ** (-8.0 * (i + 1) / n_heads) for i in range(n_heads)], dtype=np.float32))


def sliding_window_attention(q, k, v, sinks):
    B, T = q.shape[0], q.shape[1]
    pad_front = (-N_META) % BLOCK
    pad_tail = (-(T + pad_front)) % BLOCK
    Tp = T + pad_front + pad_tail
    nb = Tp // BLOCK
    pw = ((0, 0), (pad_front, pad_tail), (0, 0), (0, 0))
    qb = jnp.pad(q, pw).reshape(B, nb, BLOCK, ATT_KV_HEADS, ATT_GROUP, HEAD_DIM)
    kb = jnp.pad(k, pw).reshape(B, nb, BLOCK, ATT_KV_HEADS, HEAD_DIM)
    vb = jnp.pad(v, pw).reshape(B, nb, BLOCK, ATT_KV_HEADS, HEAD_DIM)
    zero_blk = jnp.zeros_like(kb[:, :1])
    k_band = jnp.concatenate([jnp.concatenate([zero_blk, kb[:, :-1]], axis=1), kb], axis=2)
    v_band = jnp.concatenate([jnp.concatenate([zero_blk, vb[:, :-1]], axis=1), vb], axis=2)
    k_meta = k[:, :N_META]
    v_meta = jnp.broadcast_to(v[:, None, :N_META], (B, nb, N_META, ATT_KV_HEADS, HEAD_DIM))

    meta_start = pad_front
    real_start = pad_front + N_META
    qpos = jnp.arange(Tp, dtype=jnp.int32).reshape(nb, BLOCK)
    kpos = (jnp.arange(nb, dtype=jnp.int32)[:, None] - 1) * BLOCK + jnp.arange(2 * BLOCK, dtype=jnp.int32)[None, :]
    diff = qpos[:, :, None] - kpos[:, None, :]
    band_mask = (diff >= 0) & (diff < WINDOW) & (kpos >= real_start)[:, None, :]
    mpos = meta_start + jnp.arange(N_META, dtype=jnp.int32)
    diff_m = qpos[:, :, None] - mpos[None, None, :]
    meta_mask = diff_m >= 0
    dist_m = jnp.minimum(diff_m, WINDOW)

    slopes = alibi_slopes(ATT_HEADS).reshape(1, ATT_KV_HEADS, ATT_GROUP, 1, 1)
    neg_inf = jnp.float32(-jnp.inf)
    bias_band = jnp.where(band_mask[:, None, None], -slopes * diff[:, None, None].astype(jnp.float32), neg_inf)
    bias_meta = jnp.where(meta_mask[:, None, None], -slopes * dist_m[:, None, None].astype(jnp.float32), neg_inf)

    scale = HEAD_DIM ** -0.5
    s_band = jnp.einsum('bnqhgd,bnkhd->bnhgqk', qb, k_band).astype(jnp.float32) * scale + bias_band[None]
    s_meta = jnp.einsum('bnqhgd,bmhd->bnhgqm', qb, k_meta).astype(jnp.float32) * scale + bias_meta[None]
    logits = jnp.concatenate([s_band, s_meta], axis=-1)
    sink = sinks.astype(jnp.float32).reshape(1, 1, ATT_KV_HEADS, ATT_GROUP, 1, 1)
    m = jnp.maximum(jnp.max(logits, axis=-1, keepdims=True), sink)
    p = jnp.exp(logits - m)
    probs = p / (jnp.sum(p, axis=-1, keepdims=True) + jnp.exp(sink - m))
    v_all = jnp.concatenate([v_band, v_meta], axis=2)
    o = jnp.einsum('bnhgqk,bnkhd->bnqhgd', probs.astype(v.dtype), v_all)
    o = o.reshape(B, Tp, ATT_Q)
    return o[:, pad_front:pad_front + T]


def rwkv7_time_mix(c, shift_mu, w0, w_decay_up, a0, w_aaa_up, w_gate_up, k_k, k_a, r_k, ln_x_g, ln_x_b):
    B, T, _ = c.shape
    c_prev = jnp.pad(c, ((0, 0), (1, 0), (0, 0)))[:, :-1]
    c = c + shift_mu * (c_prev - c)
    r, k, v, xw, xa, xg = jnp.split(c, np.cumsum([RWKV_DIM, RWKV_DIM, RWKV_DIM, DECAY_LORA, AAA_LORA]).tolist(), axis=-1)
    r, k, v = (t.astype(jnp.float32) for t in (r, k, v))
    w = -jax.nn.softplus(-(w0 + jnp.tanh(xw) @ w_decay_up).astype(jnp.float32)) - 0.5
    decay = jnp.exp(-jnp.exp(w))
    a = jax.nn.sigmoid((a0 + xa @ w_aaa_up).astype(jnp.float32))
    g = jax.nn.sigmoid(xg) @ w_gate_up

    hs = (B, T, RWKV_HEADS, RWKV_HEAD_DIM)
    r, k, v, decay, a = (t.reshape(hs) for t in (r, k, v, decay, a))
    k_k = k_k.astype(jnp.float32).reshape(RWKV_HEADS, RWKV_HEAD_DIM)
    k_a = k_a.astype(jnp.float32).reshape(RWKV_HEADS, RWKV_HEAD_DIM)
    kk = k * k_k
    kk = kk / jnp.maximum(jnp.sqrt(jnp.sum(kk * kk, axis=-1, keepdims=True)), 1e-12)
    k = k * (1.0 + (a - 1.0) * k_a)

    def step(S, inp):
        r_t, w_t, k_t, v_t, a_t, b_t = inp
        sa = jnp.einsum('bhij,bhj->bhi', S, a_t)
        S = S * w_t[:, :, None, :] + sa[..., None] * b_t[:, :, None, :] + v_t[..., None] * k_t[:, :, None, :]
        return S, jnp.einsum('bhij,bhj->bhi', S, r_t)

    to_t = lambda t: jnp.swapaxes(t, 0, 1)
    S0 = jnp.zeros((B, RWKV_HEADS, RWKV_HEAD_DIM, RWKV_HEAD_DIM), jnp.float32)
    _, y = lax.scan(step, S0, (to_t(r), to_t(decay), to_t(k), to_t(v), to_t(-kk), to_t(kk * a)))
    y = jnp.swapaxes(y, 0, 1)
    mu = jnp.mean(y, axis=-1, keepdims=True)
    var = jnp.mean(jnp.square(y - mu), axis=-1, keepdims=True)
    y = ((y - mu) * lax.rsqrt(var + LN_X_EPS)).reshape(B, T, RWKV_DIM)
    y = y * ln_x_g.astype(jnp.float32) + ln_x_b.astype(jnp.float32)
    bonus = jnp.sum(r * k * r_k.astype(jnp.float32), axis=-1, keepdims=True) * v
    y = y + bonus.reshape(B, T, RWKV_DIM)
    return y.astype(c.dtype) * g


def hierarchical_moe(x, w_grp, b_grp, w_exp, b_exp, e_gate, e_up, e_down):
    B, T, D = x.shape
    xf = x.reshape(-1, D)
    N = xf.shape[0]
    grp_logits = (xf @ w_grp + b_grp).astype(jnp.float32)
    p_grp = jax.nn.softmax(grp_logits, axis=-1)
    g_sel = jnp.argmax(grp_logits, axis=-1).astype(jnp.int32)
    exp_logits = (xf @ w_exp + b_exp).astype(jnp.float32).reshape(N, N_GROUPS, EXPERTS_PER_GROUP)
    in_grp = jnp.take_along_axis(exp_logits, g_sel[:, None, None], axis=1)[:, 0]
    top_v, top_i = lax.top_k(in_grp, TOP_K)
    gate = jax.nn.softmax(top_v, axis=-1) * jnp.take_along_axis(p_grp, g_sel[:, None], axis=1)
    expert_id = g_sel[:, None] * EXPERTS_PER_GROUP + top_i.astype(jnp.int32)

    A = N * TOP_K
    eid = expert_id.reshape(-1)
    tok = jnp.repeat(jnp.arange(N, dtype=jnp.int32), TOP_K)
    wt = gate.reshape(-1)
    order = jnp.argsort(eid)
    s_eid, s_tok, s_wt = eid[order], tok[order], wt[order]
    counts = jax.ops.segment_sum(jnp.ones_like(eid), eid, num_segments=N_EXPERTS)
    starts = jnp.cumsum(counts) - counts
    padded = ((counts + MOE_BLOCK - 1) // MOE_BLOCK) * MOE_BLOCK
    pends = jnp.cumsum(padded)
    pstarts = pends - padded
    dest = pstarts[s_eid] + jnp.arange(A, dtype=jnp.int32) - starts[s_eid]
    n_blocks = -(-(A + N_EXPERTS * (MOE_BLOCK - 1)) // MOE_BLOCK)
    cap = n_blocks * MOE_BLOCK
    slot_tok = jnp.full((cap,), N, dtype=jnp.int32).at[dest].set(s_tok)
    slot_wt = jnp.zeros((cap,), jnp.float32).at[dest].set(s_wt)
    block_start = jnp.arange(n_blocks, dtype=jnp.int32) * MOE_BLOCK
    block_expert = jnp.minimum(jnp.sum(block_start[:, None] >= pends[None, :], axis=1), N_EXPERTS - 1)

    x_pad = jnp.concatenate([xf, jnp.zeros((1, D), xf.dtype)], axis=0)
    xb = x_pad[slot_tok].reshape(n_blocks, MOE_BLOCK, D)

    def expert_block(args):
        xblk, e = args
        hmid = jax.nn.silu(xblk @ e_gate[e]) * (xblk @ e_up[e])
        return hmid @ e_down[e]

    yb = lax.map(expert_block, (xb, block_expert)).reshape(cap, D)
    y = jnp.zeros((N + 1, D), jnp.float32).at[slot_tok].add(yb.astype(jnp.float32) * slot_wt[:, None])
    return y[:N].astype(x.dtype).reshape(B, T, D)


def setup_inputs(seed: int = 0) -> dict:
    key = jax.random.key(seed)
    ks = jax.random.split(key, 32)
    L, D = DEPTH, D_MODEL
    nrm = lambda k, shape, s: jax.random.normal(k, shape, jnp.float32) * s
    return {
        "x": nrm(ks[0], (BATCH, SEQ, D), 1.0),
        "meta_tokens": nrm(ks[1], (N_META, D), 1.0),
        "norm_mix_g": 1.0 + nrm(ks[2], (L, D), 0.05),
        "w_in": nrm(ks[3], (L, D, IN_COLS), D ** -0.5),
        "shift_mu": jax.random.uniform(ks[4], (L, RWKV_COLS), jnp.float32),
        "w0": jax.random.uniform(ks[5], (L, RWKV_DIM), jnp.float32, minval=-6.0, maxval=-1.0),
        "w_decay_up": nrm(ks[6], (L, DECAY_LORA, RWKV_DIM), DECAY_LORA ** -0.5),
        "a0": nrm(ks[7], (L, RWKV_DIM), 0.5),
        "w_aaa_up": nrm(ks[8], (L, AAA_LORA, RWKV_DIM), 0.5 * AAA_LORA ** -0.5),
        "w_gate_up": nrm(ks[9], (L, GATE_LORA, RWKV_DIM), GATE_LORA ** -0.5),
        "k_k": 0.85 + nrm(ks[10], (L, RWKV_DIM), 0.1),
        "k_a": 1.0 + nrm(ks[11], (L, RWKV_DIM), 0.1),
        "r_k": nrm(ks[12], (L, RWKV_HEADS, RWKV_HEAD_DIM), 0.1),
        "ln_x_g": 1.0 + nrm(ks[13], (L, RWKV_DIM), 0.1),
        "ln_x_b": nrm(ks[14], (L, RWKV_DIM), 0.01),
        "attn_sinks": nrm(ks[15], (L, ATT_HEADS), 1.0),
        "w_o_attn": nrm(ks[16], (L, ATT_Q, D), ATT_Q ** -0.5),
        "w_o_rwkv": nrm(ks[17], (L, RWKV_DIM, D), RWKV_DIM ** -0.5),
        "w_out": nrm(ks[18], (L, D, D), D ** -0.5),
        "norm_ffn_g": 1.0 + nrm(ks[19], (L, D), 0.05),
        "w_grp": nrm(ks[20], (L, D, N_GROUPS), D ** -0.5),
        "b_grp": nrm(ks[21], (L, N_GROUPS), 0.01),
        "w_exp": nrm(ks[22], (L, D, N_EXPERTS), D ** -0.5),
        "b_exp": nrm(ks[23], (L, N_EXPERTS), 0.01),
        "e_gate": nrm(ks[24], (L, N_EXPERTS, D, D_EXPERT), D ** -0.5),
        "e_up": nrm(ks[25], (L, N_EXPERTS, D, D_EXPERT), D ** -0.5),
        "e_down": nrm(ks[26], (L, N_EXPERTS, D_EXPERT, D), D_EXPERT ** -0.5),
        "norm_final_g": 1.0 + nrm(ks[27], (D,), 0.05),
    }


def reference(x, meta_tokens, norm_mix_g, w_in, shift_mu, w0, w_decay_up, a0, w_aaa_up, w_gate_up,
              k_k, k_a, r_k, ln_x_g, ln_x_b, attn_sinks, w_o_attn, w_o_rwkv, w_out, norm_ffn_g,
              w_grp, b_grp, w_exp, b_exp, e_gate, e_up, e_down, norm_final_g):
    B = x.shape[0]
    meta = jnp.broadcast_to(meta_tokens.astype(x.dtype)[None], (B, N_META, D_MODEL))
    h = jnp.concatenate([meta, x], axis=1)
    T = h.shape[1]
    for l in range(DEPTH):
        xn = rmsnorm(h, norm_mix_g[l])
        cols = xn @ w_in[l]
        att_c = cols[..., :ATT_COLS]
        rwkv_c = cols[..., ATT_COLS:ATT_COLS + RWKV_COLS]
        gate_c = cols[..., ATT_COLS + RWKV_COLS:]
        q = att_c[..., :ATT_Q].reshape(B, T, ATT_HEADS, HEAD_DIM)
        k = att_c[..., ATT_Q:ATT_Q + ATT_KV].reshape(B, T, ATT_KV_HEADS, HEAD_DIM)
        v = att_c[..., ATT_Q + ATT_KV:].reshape(B, T, ATT_KV_HEADS, HEAD_DIM)
        att_out = sliding_window_attention(q, k, v, attn_sinks[l]) @ w_o_attn[l]
        rwkv_out = rwkv7_time_mix(rwkv_c, shift_mu[l], w0[l], w_decay_up[l], a0[l], w_aaa_up[l], w_gate_up[l],
                                  k_k[l], k_a[l], r_k[l], ln_x_g[l], ln_x_b[l]) @ w_o_rwkv[l]
        gates = jax.nn.sigmoid(gate_c.reshape(B, T, N_BRANCH, D_MODEL))
        merged = gates[:, :, 0] * att_out + gates[:, :, 1] * rwkv_out
        h = h + merged @ w_out[l]
        h = h + hierarchical_moe(rmsnorm(h, norm_ffn_g[l]), w_grp[l], b_grp[l], w_exp[l], b_exp[l],
                                 e_gate[l], e_up[l], e_down[l])
    h = rmsnorm(h, norm_final_g)
    return h[:, N_META:]
```

```python
import functools

import numpy as np
import jax
import jax.numpy as jnp
from jax import lax
from jax.experimental import pallas as pl
from jax.experimental.pallas import tpu as pltpu

F32 = jnp.float32
BF16 = jnp.bfloat16

D_MODEL = 1024
N_META = 16
ATT_HEADS = 8
ATT_KV_HEADS = 2
ATT_GROUP = ATT_HEADS // ATT_KV_HEADS
HEAD_DIM = 64
WINDOW = 128
BLOCK = 128
RWKV_HEADS = 8
RWKV_HEAD_DIM = 64
RWKV_DIM = RWKV_HEADS * RWKV_HEAD_DIM
DECAY_LORA = 64
AAA_LORA = 64
GATE_LORA = 128
LN_X_EPS = 64e-5
ATT_Q = ATT_HEADS * HEAD_DIM
ATT_KV = ATT_KV_HEADS * HEAD_DIM
ATT_COLS = ATT_Q + 2 * ATT_KV
RWKV_COLS = 3 * RWKV_DIM + DECAY_LORA + AAA_LORA + GATE_LORA
GATE_COLS = 2 * D_MODEL
IN_COLS = ATT_COLS + RWKV_COLS + GATE_COLS
N_GROUPS = 4
EXPERTS_PER_GROUP = 8
N_EXPERTS = N_GROUPS * EXPERTS_PER_GROUP
TOP_K = 2
D_EXPERT = 256
NORM_EPS = 1e-6

LANES = 128
CHUNK = 64
SLOT_BLOCK = 128
ROUTER_LANE0 = N_GROUPS
VMEM_LIMIT = 56 * 1024 * 1024


def _cparams(*sem):
    return pltpu.CompilerParams(dimension_semantics=sem, vmem_limit_bytes=VMEM_LIMIT)


def _dot(a, b):
    return jnp.dot(a, b, preferred_element_type=F32)


def _dot_nt(a, b):
    return lax.dot_general(a, b, (((1,), (1,)), ((), ())), preferred_element_type=F32)


def _dot_tn(a, b):
    return lax.dot_general(a, b, (((0,), (0,)), ((), ())), preferred_element_type=F32)


def _sigmoid(x):
    return 1.0 / (1.0 + jnp.exp(-x))


def _proj_body(x_ref, g_ref, w_ref, qkv_ref, rw_ref, gate_ref):
    x = x_ref[...]
    y = x * lax.rsqrt(jnp.mean(x * x, axis=-1, keepdims=True) + NORM_EPS)
    xb = (y * g_ref[...]).astype(BF16)
    qkv_ref[...] = _dot(xb, w_ref[:, :ATT_COLS]).astype(BF16)
    rw_ref[...] = _dot(xb, w_ref[:, ATT_COLS:ATT_COLS + RWKV_COLS])
    gate_ref[...] = _dot(xb, w_ref[:, ATT_COLS + RWKV_COLS:])


def _proj(x2d, g, w_bf16, tm):
    m = x2d.shape[0]
    return pl.pallas_call(
        _proj_body,
        grid=(m // tm,),
        in_specs=[
            pl.BlockSpec((tm, D_MODEL), lambda i: (i, 0)),
            pl.BlockSpec((1, D_MODEL), lambda i: (0, 0)),
            pl.BlockSpec((D_MODEL, IN_COLS), lambda i: (0, 0)),
        ],
        out_specs=[
            pl.BlockSpec((tm, ATT_COLS), lambda i: (i, 0)),
            pl.BlockSpec((tm, RWKV_COLS), lambda i: (i, 0)),
            pl.BlockSpec((tm, GATE_COLS), lambda i: (i, 0)),
        ],
        out_shape=[
            jax.ShapeDtypeStruct((m, ATT_COLS), BF16),
            jax.ShapeDtypeStruct((m, RWKV_COLS), F32),
            jax.ShapeDtypeStruct((m, GATE_COLS), F32),
        ],
        compiler_params=_cparams("parallel"),
        name="proj",
    )(x2d, g, w_bf16)


def _attn_body(sink_ref, q_ref, kc_ref, kp_ref, vc_ref, vp_ref, km_ref, vm_ref, o_ref):
    n = pl.program_id(1)
    qi = lax.broadcasted_iota(jnp.int32, (BLOCK, BLOCK), 0)
    kj = lax.broadcasted_iota(jnp.int32, (BLOCK, BLOCK), 1)
    d_cur = qi - kj
    valid_cur = d_cur >= 0
    valid_prev = jnp.logical_and(d_cur < 0, n > 0)
    f_cur = d_cur.astype(F32)
    f_prev = f_cur + float(BLOCK)
    qm = lax.broadcasted_iota(jnp.int32, (BLOCK, N_META), 0)
    mi = lax.broadcasted_iota(jnp.int32, (BLOCK, N_META), 1)
    f_meta = jnp.minimum(N_META + n * BLOCK + qm - mi, WINDOW).astype(F32)
    scale = HEAD_DIM ** -0.5
    neg_inf = jnp.float32(-jnp.inf)
    for g in range(ATT_KV_HEADS):
        ks = slice(g * HEAD_DIM, (g + 1) * HEAD_DIM)
        kc, kp, km = kc_ref[:, ks], kp_ref[:, ks], km_ref[:, ks]
        vc, vp, vm = vc_ref[:, ks], vp_ref[:, ks], vm_ref[:, ks]
        for j in range(ATT_GROUP):
            h = g * ATT_GROUP + j
            slope = float(np.float32(2.0 ** (-8.0 * (h + 1) / ATT_HEADS)))
            q = q_ref[:, h * HEAD_DIM:(h + 1) * HEAD_DIM]
            l_c = jnp.where(valid_cur, _dot_nt(q, kc) * scale - slope * f_cur, neg_inf)
            l_p = jnp.where(valid_prev, _dot_nt(q, kp) * scale - slope * f_prev, neg_inf)
            l_m = _dot_nt(q, km) * scale - slope * f_meta
            sink = sink_ref[h]
            m = jnp.maximum(jnp.maximum(jnp.max(l_c, axis=-1, keepdims=True), jnp.max(l_p, axis=-1, keepdims=True)),
                            jnp.maximum(jnp.max(l_m, axis=-1, keepdims=True), sink))
            p_c, p_p, p_m = jnp.exp(l_c - m), jnp.exp(l_p - m), jnp.exp(l_m - m)
            denom = (jnp.sum(p_c, axis=-1, keepdims=True) + jnp.sum(p_p, axis=-1, keepdims=True)
                     + jnp.sum(p_m, axis=-1, keepdims=True) + jnp.exp(sink - m))
            inv = 1.0 / denom
            o = (_dot((p_c * inv).astype(BF16), vc) + _dot((p_p * inv).astype(BF16), vp)
                 + _dot((p_m * inv).astype(BF16), vm))
            o_ref[:, h * HEAD_DIM:(h + 1) * HEAD_DIM] = o.astype(BF16)


def _attention(qkv, kmeta, vmeta, sinks):
    b, s, _ = qkv.shape
    nb = s // BLOCK
    kcol, vcol = ATT_Q // ATT_KV, ATT_Q // ATT_KV + 1
    prev = lambda n: jnp.maximum(n - 1, 0)
    return pl.pallas_call(
        _attn_body,
        grid=(b, nb),
        in_specs=[
            pl.BlockSpec(memory_space=pltpu.SMEM),
            pl.BlockSpec((None, BLOCK, ATT_Q), lambda i, n: (i, n, 0)),
            pl.BlockSpec((None, BLOCK, ATT_KV), lambda i, n: (i, n, kcol)),
            pl.BlockSpec((None, BLOCK, ATT_KV), lambda i, n: (i, prev(n), kcol)),
            pl.BlockSpec((None, BLOCK, ATT_KV), lambda i, n: (i, n, vcol)),
            pl.BlockSpec((None, BLOCK, ATT_KV), lambda i, n: (i, prev(n), vcol)),
            pl.BlockSpec((N_META, ATT_KV), lambda i, n: (0, 0)),
            pl.BlockSpec((N_META, ATT_KV), lambda i, n: (0, 0)),
        ],
        out_specs=pl.BlockSpec((None, BLOCK, ATT_Q), lambda i, n: (i, n, 0)),
        out_shape=jax.ShapeDtypeStruct((b, s, ATT_Q), BF16),
        compiler_params=_cparams("parallel", "parallel"),
        name="attn",
    )(sinks, qkv, qkv, qkv, qkv, qkv, kmeta, vmeta)


def _rwkv_body(rw_ref, cprev0_ref, s0_ref, mu_ref, w0_ref, wdec_ref, a0_ref, waaa_ref, wgate_ref,
               kk_ref, ka_ref, rk_ref, lng_ref, lnb_ref, yg_ref, sfin_ref, state_ref, prev_ref):
    ci = pl.program_id(1)
    L, N = CHUNK, RWKV_HEAD_DIM

    @pl.when(ci == 0)
    def _():
        state_ref[...] = s0_ref[...]
        prev_ref[...] = cprev0_ref[...]

    c = rw_ref[...]
    row = lax.broadcasted_iota(jnp.int32, (L, 1), 0)
    cp = jnp.where(row == 0, prev_ref[...], pltpu.roll(c, 1, 0))
    prev_ref[...] = c[L - 1:L, :]
    x = c + mu_ref[...] * (cp - c)
    r = x[:, :RWKV_DIM]
    k = x[:, RWKV_DIM:2 * RWKV_DIM]
    v = x[:, 2 * RWKV_DIM:3 * RWKV_DIM]
    o = 3 * RWKV_DIM
    xw = x[:, o:o + DECAY_LORA]
    xa = x[:, o + DECAY_LORA:o + DECAY_LORA + AAA_LORA]
    xg = x[:, o + DECAY_LORA + AAA_LORA:]

    z = -(w0_ref[...] + _dot(jnp.tanh(xw).astype(BF16), wdec_ref[...]))
    softplus = jnp.maximum(z, 0.0) + jnp.log(1.0 + jnp.exp(-jnp.abs(z)))
    lw = -jnp.exp(-softplus - 0.5)
    a = _sigmoid(a0_ref[...] + _dot(xa.astype(BF16), waaa_ref[...]))
    g = _dot(_sigmoid(xg).astype(BF16), wgate_ref[...])
    kkr = k * kk_ref[...]
    k2 = k * (1.0 + (a - 1.0) * ka_ref[...])

    ti = lax.broadcasted_iota(jnp.int32, (L, L), 0)
    tj = lax.broadcasted_iota(jnp.int32, (L, L), 1)
    tri = (ti >= tj).astype(BF16)
    lw_hi = lw.astype(BF16)
    rem = lw - lw_hi.astype(F32)
    lw_mid = rem.astype(BF16)
    lw_lo = (rem - lw_mid.astype(F32)).astype(BF16)
    cum = _dot(tri, lw_hi) + _dot(tri, lw_mid) + _dot(tri, lw_lo)
    cum_last = cum[L - 1:L, :]
    p_inc = jnp.exp(cum)
    p_inv = jnp.exp(-cum)
    p_prev = jnp.exp(cum - lw)
    p_end = jnp.exp(cum_last - cum)
    p_last = jnp.exp(cum_last)

    strict = ti > tj
    incl = ti >= tj
    eye = (ti == tj).astype(F32)
    outs = []
    for h in range(RWKV_HEADS):
        sl = slice(h * N, (h + 1) * N)
        kkh = kkr[:, sl]
        kkn = kkh / jnp.maximum(jnp.sqrt(jnp.sum(kkh * kkh, axis=-1, keepdims=True)), 1e-12)
        ah = a[:, sl]
        bv = kkn * ah
        rh, k2h, vh = r[:, sl], k2[:, sl], v[:, sl]
        a_t = (-kkn * p_prev[:, sl]).astype(BF16)
        r_t = (rh * p_inc[:, sl]).astype(BF16)
        b_t = (bv * p_inv[:, sl]).astype(BF16)
        k_t = (k2h * p_inv[:, sl]).astype(BF16)
        b_e = (bv * p_end[:, sl]).astype(BF16)
        k_e = (k2h * p_end[:, sl]).astype(BF16)
        vb = vh.astype(BF16)

        ar = jnp.concatenate([a_t, r_t], axis=0)
        sb = _dot_nt(ar, b_t)
        sk = _dot_nt(ar, k_t)
        a_ab = jnp.where(strict, sb[:L], 0.0)
        a_ak = jnp.where(strict, sk[:L], 0.0)
        a_rb = jnp.where(incl, sb[L:], 0.0)
        a_rk = jnp.where(incl, sk[L:], 0.0)

        t_inv = eye + a_ab
        apow = a_ab
        for _ in range(5):
            ab = apow.astype(BF16)
            apow = _dot(ab, ab)
            t_inv = t_inv + _dot(t_inv.astype(BF16), apow.astype(BF16))
        tb = t_inv.astype(BF16)

        s_old = state_ref[h]
        sob = s_old.astype(BF16)
        w1 = _dot(tb, a_t)
        w2 = _dot(tb, _dot(a_ak.astype(BF16), vb).astype(BF16))
        u = _dot_nt(w1.astype(BF16), sob) + w2
        ub = u.astype(BF16)
        y = _dot_nt(r_t, sob) + _dot(a_rb.astype(BF16), ub) + _dot(a_rk.astype(BF16), vb)
        upd = _dot_tn(jnp.concatenate([ub, vb], axis=0), jnp.concatenate([b_e, k_e], axis=0))
        state_ref[h] = s_old * p_last[:, sl] + upd

        mean = jnp.mean(y, axis=-1, keepdims=True)
        yc = y - mean
        var = jnp.mean(yc * yc, axis=-1, keepdims=True)
        yn = yc * lax.rsqrt(var + LN_X_EPS) * lng_ref[:, sl] + lnb_ref[:, sl]
        bonus = jnp.sum(rh * k2h * rk_ref[:, sl], axis=-1, keepdims=True) * vh
        outs.append((yn + bonus) * g[:, sl])
    yg_ref[...] = jnp.concatenate(outs, axis=1).astype(BF16)

    @pl.when(ci == pl.num_programs(1) - 1)
    def _():
        sfin_ref[...] = state_ref[...]


def _rwkv(rw, cprev0, s0, p):
    b, s, _ = rw.shape
    nc = s // CHUNK
    row = lambda n: pl.BlockSpec((1, n), lambda i, c: (0, 0))
    full = lambda shape: pl.BlockSpec(shape, lambda i, c: (0,) * len(shape))
    return pl.pallas_call(
        _rwkv_body,
        grid=(b, nc),
        in_specs=[
            pl.BlockSpec((None, CHUNK, RWKV_COLS), lambda i, c: (i, c, 0)),
            row(RWKV_COLS),
            full((RWKV_HEADS, RWKV_HEAD_DIM, RWKV_HEAD_DIM)),
            row(RWKV_COLS), row(RWKV_DIM), full((DECAY_LORA, RWKV_DIM)), row(RWKV_DIM),
            full((AAA_LORA, RWKV_DIM)), full((GATE_LORA, RWKV_DIM)),
            row(RWKV_DIM), row(RWKV_DIM), row(RWKV_DIM), row(RWKV_DIM), row(RWKV_DIM),
        ],
        out_specs=[
            pl.BlockSpec((None, CHUNK, RWKV_DIM), lambda i, c: (i, c, 0)),
            pl.BlockSpec((None, RWKV_HEADS, RWKV_HEAD_DIM, RWKV_HEAD_DIM), lambda i, c: (i, 0, 0, 0)),
        ],
        out_shape=[
            jax.ShapeDtypeStruct((b, s, RWKV_DIM), BF16),
            jax.ShapeDtypeStruct((b, RWKV_HEADS, RWKV_HEAD_DIM, RWKV_HEAD_DIM), F32),
        ],
        scratch_shapes=[
            pltpu.VMEM((RWKV_HEADS, RWKV_HEAD_DIM, RWKV_HEAD_DIM), F32),
            pltpu.VMEM((1, RWKV_COLS), F32),
        ],
        compiler_params=_cparams("parallel", "arbitrary"),
        name="rwkv",
    )(rw, cprev0, s0, p["mu"], p["w0"], p["wdec"], p["a0"], p["waaa"], p["wgate"],
      p["kk"], p["ka"], p["rk"], p["lng"], p["lnb"])


def _merge_body(x_ref, oa_ref, yg_ref, gate_ref, woa_ref, wor_ref, wout_ref, g2_ref, wr_ref, br_ref,
                h1_ref, xn_ref, info_ref, cnt_ref, base_ref):
    tm = x_ref.shape[0]

    @pl.when(pl.program_id(0) == 0)
    def _():
        base_ref[...] = jnp.zeros_like(base_ref)

    att = _dot(oa_ref[...], woa_ref[...])
    rwk = _dot(yg_ref[...], wor_ref[...])
    gate = gate_ref[...]
    merged = _sigmoid(gate[:, :D_MODEL]) * att + _sigmoid(gate[:, D_MODEL:]) * rwk
    h1 = x_ref[...] + _dot(merged.astype(BF16), wout_ref[...])
    h1_ref[...] = h1
    xn = h1 * lax.rsqrt(jnp.mean(h1 * h1, axis=-1, keepdims=True) + NORM_EPS) * g2_ref[...]
    xn_ref[...] = xn

    logits = jnp.dot(xn, wr_ref[...], preferred_element_type=F32, precision=lax.Precision.HIGHEST) + br_ref[...]
    lane = lax.broadcasted_iota(jnp.int32, (tm, LANES), 1)
    neg_inf = jnp.float32(-jnp.inf)
    big = jnp.int32(LANES)
    gl = jnp.where(lane < N_GROUPS, logits, neg_inf)
    gmax = jnp.max(gl, axis=-1, keepdims=True)
    gsel = jnp.min(jnp.where(gl == gmax, lane, big), axis=-1, keepdims=True)
    p_sel = 1.0 / jnp.sum(jnp.exp(gl - gmax), axis=-1, keepdims=True)
    lo = ROUTER_LANE0 + gsel * EXPERTS_PER_GROUP
    el = jnp.where(jnp.logical_and(lane >= lo, lane < lo + EXPERTS_PER_GROUP), logits, neg_inf)
    v1 = jnp.max(el, axis=-1, keepdims=True)
    i1 = jnp.min(jnp.where(el == v1, lane, big), axis=-1, keepdims=True)
    el2 = jnp.where(lane == i1, neg_inf, el)
    v2 = jnp.max(el2, axis=-1, keepdims=True)
    i2 = jnp.min(jnp.where(el2 == v2, lane, big), axis=-1, keepdims=True)
    e2 = jnp.exp(v2 - v1)
    w1 = p_sel / (1.0 + e2)
    w2 = p_sel * e2 / (1.0 + e2)

    oh1 = (lane == i1).astype(F32)
    oh2 = (lane == i2).astype(F32)
    both = oh1 + oh2
    ti = lax.broadcasted_iota(jnp.int32, (tm, tm), 0)
    tj = lax.broadcasted_iota(jnp.int32, (tm, tm), 1)
    before = _dot((ti > tj).astype(BF16), both.astype(BF16)) + base_ref[...]
    rank1 = jnp.sum(before * oh1, axis=-1, keepdims=True)
    rank2 = jnp.sum(before * oh2, axis=-1, keepdims=True)
    base = base_ref[...] + jnp.sum(both, axis=0, keepdims=True)
    base_ref[...] = base
    cnt_ref[...] = base

    info = jnp.where(lane == 0, (i1 - ROUTER_LANE0).astype(F32), 0.0)
    info = jnp.where(lane == 1, (i2 - ROUTER_LANE0).astype(F32), info)
    info = jnp.where(lane == 2, w1, info)
    info = jnp.where(lane == 3, w2, info)
    info = jnp.where(lane == 4, rank1, info)
    info = jnp.where(lane == 5, rank2, info)
    info_ref[...] = info


def _merge(x2d, oa, yg, gate, woa, wor, wout, g2, wr, br, tm):
    m = x2d.shape[0]
    tile = lambda n: pl.BlockSpec((tm, n), lambda i: (i, 0))
    full = lambda a, b: pl.BlockSpec((a, b), lambda i: (0, 0))
    return pl.pallas_call(
        _merge_body,
        grid=(m // tm,),
        in_specs=[
            tile(D_MODEL), tile(ATT_Q), tile(RWKV_DIM), tile(GATE_COLS),
            full(ATT_Q, D_MODEL), full(RWKV_DIM, D_MODEL), full(D_MODEL, D_MODEL),
            full(1, D_MODEL), full(D_MODEL, LANES), full(1, LANES),
        ],
        out_specs=[tile(D_MODEL), tile(D_MODEL), tile(LANES), full(1, LANES)],
        out_shape=[
            jax.ShapeDtypeStruct((m, D_MODEL), F32),
            jax.ShapeDtypeStruct((m, D_MODEL), F32),
            jax.ShapeDtypeStruct((m, LANES), F32),
            jax.ShapeDtypeStruct((1, LANES), F32),
        ],
        scratch_shapes=[pltpu.VMEM((1, LANES), F32)],
        compiler_params=_cparams("arbitrary"),
        name="merge",
    )(x2d, oa, yg, gate, woa, wor, wout, g2, wr, br)


def _dispatch_body(dest_ref, xn_ref, xs_in_ref, xs_ref, sem):
    del xs_in_ref
    tm = xn_ref.shape[0]

    def row_copy(r, d):
        return pltpu.make_async_copy(xn_ref.at[pl.ds(r, 1)], xs_ref.at[pl.ds(d, 1)], sem)

    def start(r, carry):
        row_copy(r, dest_ref[0, 0, 2 * r]).start()
        row_copy(r, dest_ref[0, 0, 2 * r + 1]).start()
        return carry

    def wait(r, carry):
        row_copy(0, 0).wait()
        row_copy(0, 0).wait()
        return carry

    lax.fori_loop(0, tm, start, 0)
    lax.fori_loop(0, tm, wait, 0)


def _dispatch(dest3, xn, xs_zero, tm):
    m = xn.shape[0]
    return pl.pallas_call(
        _dispatch_body,
        grid=(m // tm,),
        in_specs=[
            pl.BlockSpec((1, 1, 2 * tm), lambda i: (i, 0, 0), memory_space=pltpu.SMEM),
            pl.BlockSpec((tm, D_MODEL), lambda i: (i, 0)),
            pl.BlockSpec(memory_space=pl.ANY),
        ],
        out_specs=pl.BlockSpec(memory_space=pl.ANY),
        out_shape=jax.ShapeDtypeStruct(xs_zero.shape, xs_zero.dtype),
        scratch_shapes=[pltpu.SemaphoreType.DMA(())],
        input_output_aliases={2: 0},
        compiler_params=_cparams("arbitrary"),
        name="dispatch",
    )(dest3, xn, xs_zero)


def _expert_body(be_ref, xs_ref, wg_ref, wu_ref, wd_ref, yb_ref):
    del be_ref
    xb = xs_ref[...].astype(BF16)
    gt = _dot(xb, wg_ref[...])
    up = _dot(xb, wu_ref[...])
    hmid = gt * _sigmoid(gt) * up
    yb_ref[...] = _dot(hmid.astype(BF16), wd_ref[...])


def _experts(block_expert, xs, wg, wu, wd):
    cap = xs.shape[0]
    nblk = cap // SLOT_BLOCK
    return pl.pallas_call(
        _expert_body,
        grid_spec=pltpu.PrefetchScalarGridSpec(
            num_scalar_prefetch=1,
            grid=(nblk,),
            in_specs=[
                pl.BlockSpec((SLOT_BLOCK, D_MODEL), lambda i, be: (i, 0)),
                pl.BlockSpec((None, D_MODEL, D_EXPERT), lambda i, be: (be[i], 0, 0)),
                pl.BlockSpec((None, D_MODEL, D_EXPERT), lambda i, be: (be[i], 0, 0)),
                pl.BlockSpec((None, D_EXPERT, D_MODEL), lambda i, be: (be[i], 0, 0)),
            ],
            out_specs=pl.BlockSpec((SLOT_BLOCK, D_MODEL), lambda i, be: (i, 0)),
        ),
        out_shape=jax.ShapeDtypeStruct((cap, D_MODEL), F32),
        compiler_params=_cparams("arbitrary"),
        name="experts",
    )(block_expert, xs, wg, wu, wd)


def _combine_body(dest_ref, h1_ref, info_ref, gf_ref, yb_ref, out_ref, buf_ref, sem):
    tm = h1_ref.shape[0]

    def row_copy(d, k, r):
        return pltpu.make_async_copy(yb_ref.at[pl.ds(d, 1)], buf_ref.at[k, pl.ds(r, 1)], sem)

    def start(r, carry):
        row_copy(dest_ref[0, 0, 2 * r], 0, r).start()
        row_copy(dest_ref[0, 0, 2 * r + 1], 1, r).start()
        return carry

    def wait(r, carry):
        row_copy(0, 0, 0).wait()
        row_copy(0, 0, 0).wait()
        return carry

    lax.fori_loop(0, tm, start, 0)
    lax.fori_loop(0, tm, wait, 0)
    info = info_ref[...]
    y = buf_ref[0] * info[:, 2:3] + buf_ref[1] * info[:, 3:4]
    h2 = h1_ref[...] + y
    out_ref[...] = h2 * lax.rsqrt(jnp.mean(h2 * h2, axis=-1, keepdims=True) + NORM_EPS) * gf_ref[...]


def _combine(dest3, h1, info, gf, yb, tm):
    m = h1.shape[0]
    return pl.pallas_call(
        _combine_body,
        grid=(m // tm,),
        in_specs=[
            pl.BlockSpec((1, 1, 2 * tm), lambda i: (i, 0, 0), memory_space=pltpu.SMEM),
            pl.BlockSpec((tm, D_MODEL), lambda i: (i, 0)),
            pl.BlockSpec((tm, LANES), lambda i: (i, 0)),
            pl.BlockSpec((1, D_MODEL), lambda i: (0, 0)),
            pl.BlockSpec(memory_space=pl.ANY),
        ],
        out_specs=pl.BlockSpec((tm, D_MODEL), lambda i: (i, 0)),
        out_shape=jax.ShapeDtypeStruct((m, D_MODEL), F32),
        scratch_shapes=[pltpu.VMEM((2, tm, D_MODEL), F32), pltpu.SemaphoreType.DMA(())],
        compiler_params=_cparams("arbitrary"),
        name="combine",
    )(dest3, h1, info, gf, yb)


def _tile(m, pref):
    t = pref
    while m % t:
        t //= 2
    return t


def kernel(x, meta_tokens, norm_mix_g, w_in, shift_mu, w0, w_decay_up, a0, w_aaa_up, w_gate_up, k_k, k_a, r_k,
           ln_x_g, ln_x_b, attn_sinks, w_o_attn, w_o_rwkv, w_out, norm_ffn_g, w_grp, b_grp, w_exp, b_exp,
           e_gate, e_up, e_down, norm_final_g):
    assert norm_mix_g.shape[0] == 1, "single-layer trunk"
    b, s, d = x.shape
    assert d == D_MODEL and s % BLOCK == 0
    m = b * s
    row = lambda t: t.reshape(1, -1).astype(F32)
    w_in_b = w_in[0].astype(BF16)
    g_mix = row(norm_mix_g[0])
    rp = dict(mu=row(shift_mu[0]), w0=row(w0[0]), wdec=w_decay_up[0].astype(BF16), a0=row(a0[0]),
              waaa=w_aaa_up[0].astype(BF16), wgate=w_gate_up[0].astype(BF16), kk=row(k_k[0]), ka=row(k_a[0]),
              rk=row(r_k[0]), lng=row(ln_x_g[0]), lnb=row(ln_x_b[0]))

    x0 = jnp.concatenate([jnp.zeros((BLOCK - N_META, d), F32), meta_tokens.astype(F32)], axis=0)
    qkv0, rw0, _ = _proj(x0, g_mix, w_in_b, BLOCK)
    kmeta = qkv0[BLOCK - N_META:, ATT_Q:ATT_Q + ATT_KV]
    vmeta = qkv0[BLOCK - N_META:, ATT_Q + ATT_KV:]
    zero_state = jnp.zeros((RWKV_HEADS, RWKV_HEAD_DIM, RWKV_HEAD_DIM), F32)
    _, s_meta = _rwkv(rw0[None], jnp.zeros((1, RWKV_COLS), F32), zero_state, rp)

    x2d = x.reshape(m, d)
    qkv, rw, gate = _proj(x2d, g_mix, w_in_b, _tile(m, 256))
    o_att = _attention(qkv.reshape(b, s, ATT_COLS), kmeta, vmeta, attn_sinks[0].astype(F32))
    yg, _ = _rwkv(rw.reshape(b, s, RWKV_COLS), rw0[BLOCK - 1:], s_meta[0], rp)

    wr = jnp.zeros((d, LANES), F32).at[:, :N_GROUPS].set(w_grp[0]).at[:, N_GROUPS:N_GROUPS + N_EXPERTS].set(w_exp[0])
    br = jnp.zeros((1, LANES), F32).at[0, :N_GROUPS].set(b_grp[0]).at[0, N_GROUPS:N_GROUPS + N_EXPERTS].set(b_exp[0])
    tm = _tile(m, 256)
    h1, xn2, info, cnt = _merge(x2d, o_att.reshape(m, ATT_Q), yg.reshape(m, RWKV_DIM), gate,
                                w_o_attn[0].astype(BF16), w_o_rwkv[0].astype(BF16), w_out[0].astype(BF16),
                                row(norm_ffn_g[0]), wr, br, tm)

    counts = cnt[0, N_GROUPS:N_GROUPS + N_EXPERTS].astype(jnp.int32)
    padded = ((counts + SLOT_BLOCK - 1) // SLOT_BLOCK) * SLOT_BLOCK
    pends = jnp.cumsum(padded)
    pstarts = pends - padded
    eid = info[:, 0:2].astype(jnp.int32)
    dest = pstarts[eid] + info[:, 4:6].astype(jnp.int32)
    nblk = -(-(m * TOP_K + N_EXPERTS * (SLOT_BLOCK - 1)) // SLOT_BLOCK)
    cap = nblk * SLOT_BLOCK
    block_start = jnp.arange(nblk, dtype=jnp.int32) * SLOT_BLOCK
    block_expert = jnp.minimum(jnp.sum(block_start[:, None] >= pends[None, :], axis=1), N_EXPERTS - 1).astype(jnp.int32)
    dest3 = dest.reshape(m // tm, 1, 2 * tm)

    xs = _dispatch(dest3, xn2, jnp.zeros((cap, d), F32), tm)
    yb = _experts(block_expert, xs, e_gate[0].astype(BF16), e_up[0].astype(BF16), e_down[0].astype(BF16))
    out = _combine(dest3, h1, info, row(norm_final_g), yb, tm)
    return out.reshape(b, s, d)
```

```python
import functools

import numpy as np
import jax
import jax.numpy as jnp
from jax import lax
from jax.experimental import pallas as pl
from jax.experimental.pallas import tpu as pltpu

F32 = jnp.float32
BF16 = jnp.bfloat16

D_MODEL = 1024
N_META = 16
ATT_HEADS = 8
ATT_KV_HEADS = 2
ATT_GROUP = ATT_HEADS // ATT_KV_HEADS
HEAD_DIM = 64
WINDOW = 128
BLOCK = 128
RWKV_HEADS = 8
RWKV_HEAD_DIM = 64
RWKV_DIM = RWKV_HEADS * RWKV_HEAD_DIM
DECAY_LORA = 64
AAA_LORA = 64
GATE_LORA = 128
LN_X_EPS = 64e-5
ATT_Q = ATT_HEADS * HEAD_DIM
ATT_KV = ATT_KV_HEADS * HEAD_DIM
ATT_COLS = ATT_Q + 2 * ATT_KV
RWKV_COLS = 3 * RWKV_DIM + DECAY_LORA + AAA_LORA + GATE_LORA
GATE_COLS = 2 * D_MODEL
IN_COLS = ATT_COLS + RWKV_COLS + GATE_COLS
N_GROUPS = 4
EXPERTS_PER_GROUP = 8
N_EXPERTS = N_GROUPS * EXPERTS_PER_GROUP
TOP_K = 2
D_EXPERT = 256
NORM_EPS = 1e-6

LANES = 128
CHUNK = 64
SLOT_BLOCK = 256
ROW_TILE = 512
DMA_TILE = 256
DMA_UNROLL = 8
ROUTER_LANE0 = N_GROUPS
VMEM_LIMIT = 56 * 1024 * 1024


def _cparams(*sem):
    return pltpu.CompilerParams(dimension_semantics=sem, vmem_limit_bytes=VMEM_LIMIT)


def _dot(a, b):
    return jnp.dot(a, b, preferred_element_type=F32)


def _dot_nt(a, b):
    return lax.dot_general(a, b, (((1,), (1,)), ((), ())), preferred_element_type=F32)


def _dot_tn(a, b):
    return lax.dot_general(a, b, (((0,), (0,)), ((), ())), preferred_element_type=F32)


def _sigmoid(x):
    return 0.5 * jnp.tanh(0.5 * x) + 0.5


def _proj_body(x_ref, g_ref, w_ref, qkv_ref, rw_ref, gate_ref):
    x = x_ref[...]
    y = x * lax.rsqrt(jnp.mean(x * x, axis=-1, keepdims=True) + NORM_EPS)
    xb = (y * g_ref[...]).astype(BF16)
    qkv_ref[...] = _dot(xb, w_ref[:, :ATT_COLS]).astype(BF16)
    rw_ref[...] = _dot(xb, w_ref[:, ATT_COLS:ATT_COLS + RWKV_COLS])
    gate_ref[...] = _dot(xb, w_ref[:, ATT_COLS + RWKV_COLS:])


def _proj(x2d, g, w_bf16, tm):
    m = x2d.shape[0]
    return pl.pallas_call(
        _proj_body,
        grid=(m // tm,),
        in_specs=[
            pl.BlockSpec((tm, D_MODEL), lambda i: (i, 0)),
            pl.BlockSpec((1, D_MODEL), lambda i: (0, 0)),
            pl.BlockSpec((D_MODEL, IN_COLS), lambda i: (0, 0)),
        ],
        out_specs=[
            pl.BlockSpec((tm, ATT_COLS), lambda i: (i, 0)),
            pl.BlockSpec((tm, RWKV_COLS), lambda i: (i, 0)),
            pl.BlockSpec((tm, GATE_COLS), lambda i: (i, 0)),
        ],
        out_shape=[
            jax.ShapeDtypeStruct((m, ATT_COLS), BF16),
            jax.ShapeDtypeStruct((m, RWKV_COLS), F32),
            jax.ShapeDtypeStruct((m, GATE_COLS), F32),
        ],
        compiler_params=_cparams("parallel"),
        name="proj",
    )(x2d, g, w_bf16)


def _attn_body(sink_ref, q_ref, kc_ref, kp_ref, vc_ref, vp_ref, km_ref, vm_ref, o_ref):
    n = pl.program_id(1)
    rows = ATT_GROUP * BLOCK
    ri = lax.broadcasted_iota(jnp.int32, (rows, BLOCK), 0)
    kj = lax.broadcasted_iota(jnp.int32, (rows, BLOCK), 1)
    d_cur = (ri % BLOCK) - kj
    valid_cur = d_cur >= 0
    valid_prev = jnp.logical_and(d_cur < 0, n > 0)
    f_cur = d_cur.astype(F32)
    f_prev = f_cur + float(BLOCK)
    rm = lax.broadcasted_iota(jnp.int32, (rows, N_META), 0)
    mi = lax.broadcasted_iota(jnp.int32, (rows, N_META), 1)
    f_meta = jnp.minimum(N_META + n * BLOCK + (rm % BLOCK) - mi, WINDOW).astype(F32)
    head_in_group = lax.broadcasted_iota(jnp.int32, (rows, 1), 0) // BLOCK
    scale = HEAD_DIM ** -0.5
    neg_inf = jnp.float32(-jnp.inf)
    groups = range(ATT_KV_HEADS)
    cols = [slice(g * HEAD_DIM, (g + 1) * HEAD_DIM) for g in groups]

    slope, sink = [], []
    for g in groups:
        sl_g = jnp.zeros((rows, 1), F32)
        sk_g = jnp.zeros((rows, 1), F32)
        for j in range(ATT_GROUP):
            h = g * ATT_GROUP + j
            sl_g = jnp.where(head_in_group == j, float(np.float32(2.0 ** (-8.0 * (h + 1) / ATT_HEADS))), sl_g)
            sk_g = jnp.where(head_in_group == j, sink_ref[h], sk_g)
        slope.append(sl_g)
        sink.append(sk_g)

    q4 = [jnp.concatenate([q_ref[:, (g * ATT_GROUP + j) * HEAD_DIM:(g * ATT_GROUP + j + 1) * HEAD_DIM]
                           for j in range(ATT_GROUP)], axis=0) for g in groups]
    s_c = [_dot_nt(q4[g], kc_ref[:, cols[g]]) for g in groups]
    s_p = [_dot_nt(q4[g], kp_ref[:, cols[g]]) for g in groups]
    s_m = [_dot_nt(q4[g], km_ref[:, cols[g]]) for g in groups]
    l_c = [jnp.where(valid_cur, s_c[g] * scale - slope[g] * f_cur, neg_inf) for g in groups]
    l_p = [jnp.where(valid_prev, s_p[g] * scale - slope[g] * f_prev, neg_inf) for g in groups]
    l_m = [s_m[g] * scale - slope[g] * f_meta for g in groups]
    m = [jnp.maximum(jnp.maximum(jnp.max(jnp.maximum(l_c[g], l_p[g]), axis=-1, keepdims=True),
                                 jnp.max(l_m[g], axis=-1, keepdims=True)), sink[g]) for g in groups]
    p_c = [jnp.exp(l_c[g] - m[g]) for g in groups]
    p_p = [jnp.exp(l_p[g] - m[g]) for g in groups]
    p_m = [jnp.exp(l_m[g] - m[g]) for g in groups]
    inv = [1.0 / (jnp.sum(p_c[g] + p_p[g], axis=-1, keepdims=True) + jnp.sum(p_m[g], axis=-1, keepdims=True)
                  + jnp.exp(sink[g] - m[g])) for g in groups]
    o4 = [(_dot((p_c[g] * inv[g]).astype(BF16), vc_ref[:, cols[g]])
           + _dot((p_p[g] * inv[g]).astype(BF16), vp_ref[:, cols[g]])
           + _dot((p_m[g] * inv[g]).astype(BF16), vm_ref[:, cols[g]])).astype(BF16) for g in groups]
    o_ref[...] = jnp.concatenate([o4[g][j * BLOCK:(j + 1) * BLOCK] for g in groups for j in range(ATT_GROUP)], axis=1)


def _attention(qkv, kmeta, vmeta, sinks):
    b, s, _ = qkv.shape
    nb = s // BLOCK
    kcol, vcol = ATT_Q // ATT_KV, ATT_Q // ATT_KV + 1
    prev = lambda n: jnp.maximum(n - 1, 0)
    return pl.pallas_call(
        _attn_body,
        grid=(b, nb),
        in_specs=[
            pl.BlockSpec(memory_space=pltpu.SMEM),
            pl.BlockSpec((None, BLOCK, ATT_Q), lambda i, n: (i, n, 0)),
            pl.BlockSpec((None, BLOCK, ATT_KV), lambda i, n: (i, n, kcol)),
            pl.BlockSpec((None, BLOCK, ATT_KV), lambda i, n: (i, prev(n), kcol)),
            pl.BlockSpec((None, BLOCK, ATT_KV), lambda i, n: (i, n, vcol)),
            pl.BlockSpec((None, BLOCK, ATT_KV), lambda i, n: (i, prev(n), vcol)),
            pl.BlockSpec((N_META, ATT_KV), lambda i, n: (0, 0)),
            pl.BlockSpec((N_META, ATT_KV), lambda i, n: (0, 0)),
        ],
        out_specs=pl.BlockSpec((None, BLOCK, ATT_Q), lambda i, n: (i, n, 0)),
        out_shape=jax.ShapeDtypeStruct((b, s, ATT_Q), BF16),
        compiler_params=_cparams("parallel", "parallel"),
        name="attn",
    )(sinks, qkv, qkv, qkv, qkv, qkv, kmeta, vmeta)


def _rwkv_body(rw_ref, cprev0_ref, s0_ref, mu_ref, w0_ref, wdec_ref, a0_ref, waaa_ref, wgate_ref,
               kk_ref, ka_ref, rk_ref, lng_ref, lnb_ref, yg_ref, sfin_ref, state_ref, prev_ref):
    ci = pl.program_id(1)
    L, N = CHUNK, RWKV_HEAD_DIM

    @pl.when(ci == 0)
    def _():
        state_ref[...] = s0_ref[...]
        prev_ref[...] = cprev0_ref[...]

    c = rw_ref[...]
    row = lax.broadcasted_iota(jnp.int32, (L, 1), 0)
    cp = jnp.where(row == 0, prev_ref[...], pltpu.roll(c, 1, 0))
    prev_ref[...] = c[L - 1:L, :]
    x = c + mu_ref[...] * (cp - c)
    r = x[:, :RWKV_DIM]
    k = x[:, RWKV_DIM:2 * RWKV_DIM]
    v = x[:, 2 * RWKV_DIM:3 * RWKV_DIM]
    o = 3 * RWKV_DIM
    xw = x[:, o:o + DECAY_LORA]
    xa = x[:, o + DECAY_LORA:o + DECAY_LORA + AAA_LORA]
    xg = x[:, o + DECAY_LORA + AAA_LORA:]

    z = -(w0_ref[...] + _dot(jnp.tanh(xw).astype(BF16), wdec_ref[...]))
    softplus = jnp.maximum(z, 0.0) + jnp.log(1.0 + jnp.exp(-jnp.abs(z)))
    lw = -jnp.exp(-softplus - 0.5)
    a = _sigmoid(a0_ref[...] + _dot(xa.astype(BF16), waaa_ref[...]))
    g = _dot(_sigmoid(xg).astype(BF16), wgate_ref[...])
    kkr = k * kk_ref[...]
    k2 = k * (1.0 + (a - 1.0) * ka_ref[...])

    ti = lax.broadcasted_iota(jnp.int32, (L, L), 0)
    tj = lax.broadcasted_iota(jnp.int32, (L, L), 1)
    tri = (ti >= tj).astype(BF16)
    lw_hi = lw.astype(BF16)
    rem = lw - lw_hi.astype(F32)
    lw_mid = rem.astype(BF16)
    lw_lo = (rem - lw_mid.astype(F32)).astype(BF16)
    cum = _dot(tri, lw_hi) + _dot(tri, lw_mid) + _dot(tri, lw_lo)
    cum_last = cum[L - 1:L, :]
    p_inc = jnp.exp(cum)
    p_inv = jnp.exp(-cum)
    p_prev = jnp.exp(cum - lw)
    p_end = jnp.exp(cum_last - cum)
    p_last = jnp.exp(cum_last)

    strict = ti > tj
    incl = ti >= tj
    eye = (ti == tj).astype(F32)
    mask_akrk = jnp.concatenate([strict, incl], axis=0)
    heads = range(RWKV_HEADS)
    sls = [slice(h * N, (h + 1) * N) for h in heads]

    kkh = [kkr[:, sl] for sl in sls]
    kkn = [x / jnp.maximum(jnp.sqrt(jnp.sum(x * x, axis=-1, keepdims=True)), 1e-12) for x in kkh]
    bv = [kkn[h] * a[:, sls[h]] for h in heads]
    rh = [r[:, sl] for sl in sls]
    k2h = [k2[:, sl] for sl in sls]
    vh = [v[:, sl] for sl in sls]
    vb = [x.astype(BF16) for x in vh]
    ar = [jnp.concatenate([(-kkn[h] * p_prev[:, sls[h]]).astype(BF16), (rh[h] * p_inc[:, sls[h]]).astype(BF16)],
                          axis=0) for h in heads]
    b_t = [(bv[h] * p_inv[:, sls[h]]).astype(BF16) for h in heads]
    k_t = [(k2h[h] * p_inv[:, sls[h]]).astype(BF16) for h in heads]
    bk_e = [jnp.concatenate([(bv[h] * p_end[:, sls[h]]).astype(BF16), (k2h[h] * p_end[:, sls[h]]).astype(BF16)],
                            axis=0) for h in heads]
    s_old = [state_ref[h] for h in heads]
    sob = [x.astype(BF16) for x in s_old]

    sb = [_dot_nt(ar[h], b_t[h]) for h in heads]
    sk = [_dot_nt(ar[h], k_t[h]) for h in heads]
    ars = [_dot_nt(ar[h], sob[h]) for h in heads]
    a_ab = [jnp.where(strict, sb[h][:L], 0.0) for h in heads]
    a_rb = [jnp.where(incl, sb[h][L:], 0.0).astype(BF16) for h in heads]
    akrk = [jnp.where(mask_akrk, sk[h], 0.0).astype(BF16) for h in heads]
    akv = [_dot(akrk[h], vb[h]) for h in heads]

    t_inv = [eye + a_ab[h] for h in heads]
    ab = [x.astype(BF16) for x in a_ab]
    pk = [_dot(ab[h], ab[h]) for h in heads]
    for step in range(5):
        pkb = [x.astype(BF16) for x in pk]
        t_inv = [t_inv[h] + _dot(t_inv[h].astype(BF16), pkb[h]) for h in heads]
        if step < 4:
            pk = [_dot(pkb[h], pkb[h]) for h in heads]

    z = [(ars[h][:L] + akv[h][:L]).astype(BF16) for h in heads]
    ub = [_dot(t_inv[h].astype(BF16), z[h]).astype(BF16) for h in heads]
    y = [ars[h][L:] + akv[h][L:] + _dot(a_rb[h], ub[h]) for h in heads]
    upd = [_dot_tn(jnp.concatenate([ub[h], vb[h]], axis=0), bk_e[h]) for h in heads]
    for h in heads:
        state_ref[h] = s_old[h] * p_last[:, sls[h]] + upd[h]

    outs = []
    for h in heads:
        mean = jnp.mean(y[h], axis=-1, keepdims=True)
        yc = y[h] - mean
        var = jnp.mean(yc * yc, axis=-1, keepdims=True)
        yn = yc * lax.rsqrt(var + LN_X_EPS) * lng_ref[:, sls[h]] + lnb_ref[:, sls[h]]
        bonus = jnp.sum(rh[h] * k2h[h] * rk_ref[:, sls[h]], axis=-1, keepdims=True) * vh[h]
        outs.append((yn + bonus) * g[:, sls[h]])
    yg_ref[...] = jnp.concatenate(outs, axis=1).astype(BF16)

    @pl.when(ci == pl.num_programs(1) - 1)
    def _():
        sfin_ref[...] = state_ref[...]


def _rwkv(rw, cprev0, s0, p):
    b, s, _ = rw.shape
    nc = s // CHUNK
    row = lambda n: pl.BlockSpec((1, n), lambda i, c: (0, 0))
    full = lambda shape: pl.BlockSpec(shape, lambda i, c: (0,) * len(shape))
    return pl.pallas_call(
        _rwkv_body,
        grid=(b, nc),
        in_specs=[
            pl.BlockSpec((None, CHUNK, RWKV_COLS), lambda i, c: (i, c, 0)),
            row(RWKV_COLS),
            full((RWKV_HEADS, RWKV_HEAD_DIM, RWKV_HEAD_DIM)),
            row(RWKV_COLS), row(RWKV_DIM), full((DECAY_LORA, RWKV_DIM)), row(RWKV_DIM),
            full((AAA_LORA, RWKV_DIM)), full((GATE_LORA, RWKV_DIM)),
            row(RWKV_DIM), row(RWKV_DIM), row(RWKV_DIM), row(RWKV_DIM), row(RWKV_DIM),
        ],
        out_specs=[
            pl.BlockSpec((None, CHUNK, RWKV_DIM), lambda i, c: (i, c, 0)),
            pl.BlockSpec((None, RWKV_HEADS, RWKV_HEAD_DIM, RWKV_HEAD_DIM), lambda i, c: (i, 0, 0, 0)),
        ],
        out_shape=[
            jax.ShapeDtypeStruct((b, s, RWKV_DIM), BF16),
            jax.ShapeDtypeStruct((b, RWKV_HEADS, RWKV_HEAD_DIM, RWKV_HEAD_DIM), F32),
        ],
        scratch_shapes=[
            pltpu.VMEM((RWKV_HEADS, RWKV_HEAD_DIM, RWKV_HEAD_DIM), F32),
            pltpu.VMEM((1, RWKV_COLS), F32),
        ],
        compiler_params=_cparams("parallel", "arbitrary"),
        name="rwkv",
    )(rw, cprev0, s0, p["mu"], p["w0"], p["wdec"], p["a0"], p["waaa"], p["wgate"],
      p["kk"], p["ka"], p["rk"], p["lng"], p["lnb"])


def _merge_body(x_ref, oa_ref, yg_ref, gate_ref, woa_ref, wor_ref, wout_ref, g2_ref, wrh_ref, wrl_ref, br_ref,
                h1_ref, xn_ref, info_ref, cnt_ref, base_ref):
    tm = x_ref.shape[0]

    @pl.when(pl.program_id(0) == 0)
    def _():
        base_ref[...] = jnp.zeros_like(base_ref)

    att = _dot(oa_ref[...], woa_ref[...])
    rwk = _dot(yg_ref[...], wor_ref[...])
    gate = gate_ref[...]
    merged = _sigmoid(gate[:, :D_MODEL]) * att + _sigmoid(gate[:, D_MODEL:]) * rwk
    h1 = x_ref[...] + _dot(merged.astype(BF16), wout_ref[...])
    h1_ref[...] = h1
    xn = h1 * lax.rsqrt(jnp.mean(h1 * h1, axis=-1, keepdims=True) + NORM_EPS) * g2_ref[...]
    xn_ref[...] = xn

    xn_hi = xn.astype(BF16)
    xn_lo = (xn - xn_hi.astype(F32)).astype(BF16)
    logits = (_dot(xn_hi, wrh_ref[...]) + _dot(xn_lo, wrh_ref[...]) + _dot(xn_hi, wrl_ref[...])) + br_ref[...]
    lane = lax.broadcasted_iota(jnp.int32, (tm, LANES), 1)
    neg_inf = jnp.float32(-jnp.inf)
    big = jnp.int32(LANES)
    gl = jnp.where(lane < N_GROUPS, logits, neg_inf)
    gmax = jnp.max(gl, axis=-1, keepdims=True)
    gsel = jnp.min(jnp.where(gl == gmax, lane, big), axis=-1, keepdims=True)
    p_sel = 1.0 / jnp.sum(jnp.exp(gl - gmax), axis=-1, keepdims=True)
    lo = ROUTER_LANE0 + gsel * EXPERTS_PER_GROUP
    el = jnp.where(jnp.logical_and(lane >= lo, lane < lo + EXPERTS_PER_GROUP), logits, neg_inf)
    v1 = jnp.max(el, axis=-1, keepdims=True)
    i1 = jnp.min(jnp.where(el == v1, lane, big), axis=-1, keepdims=True)
    el2 = jnp.where(lane == i1, neg_inf, el)
    v2 = jnp.max(el2, axis=-1, keepdims=True)
    i2 = jnp.min(jnp.where(el2 == v2, lane, big), axis=-1, keepdims=True)
    e2 = jnp.exp(v2 - v1)
    w1 = p_sel / (1.0 + e2)
    w2 = p_sel * e2 / (1.0 + e2)

    oh1 = (lane == i1).astype(F32)
    oh2 = (lane == i2).astype(F32)
    both = oh1 + oh2
    ti = lax.broadcasted_iota(jnp.int32, (tm, tm), 0)
    tj = lax.broadcasted_iota(jnp.int32, (tm, tm), 1)
    before = _dot((ti > tj).astype(BF16), both.astype(BF16)) + base_ref[...]
    rank1 = jnp.sum(before * oh1, axis=-1, keepdims=True)
    rank2 = jnp.sum(before * oh2, axis=-1, keepdims=True)
    base = base_ref[...] + jnp.sum(both, axis=0, keepdims=True)
    base_ref[...] = base
    cnt_ref[...] = base

    info = jnp.where(lane == 0, (i1 - ROUTER_LANE0).astype(F32), 0.0)
    info = jnp.where(lane == 1, (i2 - ROUTER_LANE0).astype(F32), info)
    info = jnp.where(lane == 2, w1, info)
    info = jnp.where(lane == 3, w2, info)
    info = jnp.where(lane == 4, rank1, info)
    info = jnp.where(lane == 5, rank2, info)
    info_ref[...] = info


def _merge(x2d, oa, yg, gate, woa, wor, wout, g2, wr, br, tm):
    wr_hi = wr.astype(BF16)
    wr_lo = (wr - wr_hi.astype(F32)).astype(BF16)
    m = x2d.shape[0]
    tile = lambda n: pl.BlockSpec((tm, n), lambda i: (i, 0))
    full = lambda a, b: pl.BlockSpec((a, b), lambda i: (0, 0))
    return pl.pallas_call(
        _merge_body,
        grid=(m // tm,),
        in_specs=[
            tile(D_MODEL), tile(ATT_Q), tile(RWKV_DIM), tile(GATE_COLS),
            full(ATT_Q, D_MODEL), full(RWKV_DIM, D_MODEL), full(D_MODEL, D_MODEL),
            full(1, D_MODEL), full(D_MODEL, LANES), full(D_MODEL, LANES), full(1, LANES),
        ],
        out_specs=[tile(D_MODEL), tile(D_MODEL), tile(LANES), full(1, LANES)],
        out_shape=[
            jax.ShapeDtypeStruct((m, D_MODEL), F32),
            jax.ShapeDtypeStruct((m, D_MODEL), F32),
            jax.ShapeDtypeStruct((m, LANES), F32),
            jax.ShapeDtypeStruct((1, LANES), F32),
        ],
        scratch_shapes=[pltpu.VMEM((1, LANES), F32)],
        compiler_params=_cparams("arbitrary"),
        name="merge",
    )(x2d, oa, yg, gate, woa, wor, wout, g2, wr_hi, wr_lo, br)


def _dispatch_body(dest_ref, xn_ref, xs_in_ref, xs_ref, sem):
    del xs_in_ref
    tm = xn_ref.shape[0]

    def row_copy(r, d):
        return pltpu.make_async_copy(xn_ref.at[pl.ds(r, 1)], xs_ref.at[pl.ds(d, 1)], sem)

    def start(r, carry):
        row_copy(r, dest_ref[0, 0, 2 * r]).start()
        row_copy(r, dest_ref[0, 0, 2 * r + 1]).start()
        return carry

    def wait(r, carry):
        row_copy(0, 0).wait()
        row_copy(0, 0).wait()
        return carry

    lax.fori_loop(0, tm, start, 0, unroll=DMA_UNROLL)
    lax.fori_loop(0, tm, wait, 0, unroll=DMA_UNROLL)


def _dispatch(dest3, xn, xs_zero, tm):
    m = xn.shape[0]
    return pl.pallas_call(
        _dispatch_body,
        grid=(m // tm,),
        in_specs=[
            pl.BlockSpec((1, 1, 2 * tm), lambda i: (i, 0, 0), memory_space=pltpu.SMEM),
            pl.BlockSpec((tm, D_MODEL), lambda i: (i, 0)),
            pl.BlockSpec(memory_space=pl.ANY),
        ],
        out_specs=pl.BlockSpec(memory_space=pl.ANY),
        out_shape=jax.ShapeDtypeStruct(xs_zero.shape, xs_zero.dtype),
        scratch_shapes=[pltpu.SemaphoreType.DMA(())],
        input_output_aliases={2: 0},
        compiler_params=_cparams("arbitrary"),
        name="dispatch",
    )(dest3, xn, xs_zero)


def _expert_body(be_ref, xs_ref, wg_ref, wu_ref, wd_ref, yb_ref):
    del be_ref
    xb = xs_ref[...].astype(BF16)
    gt = _dot(xb, wg_ref[...])
    up = _dot(xb, wu_ref[...])
    hmid = gt * _sigmoid(gt) * up
    yb_ref[...] = _dot(hmid.astype(BF16), wd_ref[...])


def _experts(block_expert, xs, wg, wu, wd):
    cap = xs.shape[0]
    nblk = cap // SLOT_BLOCK
    return pl.pallas_call(
        _expert_body,
        grid_spec=pltpu.PrefetchScalarGridSpec(
            num_scalar_prefetch=1,
            grid=(nblk,),
            in_specs=[
                pl.BlockSpec((SLOT_BLOCK, D_MODEL), lambda i, be: (i, 0)),
                pl.BlockSpec((None, D_MODEL, D_EXPERT), lambda i, be: (be[i], 0, 0)),
                pl.BlockSpec((None, D_MODEL, D_EXPERT), lambda i, be: (be[i], 0, 0)),
                pl.BlockSpec((None, D_EXPERT, D_MODEL), lambda i, be: (be[i], 0, 0)),
            ],
            out_specs=pl.BlockSpec((SLOT_BLOCK, D_MODEL), lambda i, be: (i, 0)),
        ),
        out_shape=jax.ShapeDtypeStruct((cap, D_MODEL), F32),
        compiler_params=_cparams("arbitrary"),
        name="experts",
    )(block_expert, xs, wg, wu, wd)


def _combine_body(dest_ref, h1_ref, info_ref, gf_ref, yb_ref, out_ref, buf_ref, sem):
    tm = h1_ref.shape[0]

    def row_copy(d, k, r):
        return pltpu.make_async_copy(yb_ref.at[pl.ds(d, 1)], buf_ref.at[k, pl.ds(r, 1)], sem)

    def start(r, carry):
        row_copy(dest_ref[0, 0, 2 * r], 0, r).start()
        row_copy(dest_ref[0, 0, 2 * r + 1], 1, r).start()
        return carry

    def wait(r, carry):
        row_copy(0, 0, 0).wait()
        row_copy(0, 0, 0).wait()
        return carry

    lax.fori_loop(0, tm, start, 0, unroll=DMA_UNROLL)
    lax.fori_loop(0, tm, wait, 0, unroll=DMA_UNROLL)
    info = info_ref[...]
    y = buf_ref[0] * info[:, 2:3] + buf_ref[1] * info[:, 3:4]
    h2 = h1_ref[...] + y
    out_ref[...] = h2 * lax.rsqrt(jnp.mean(h2 * h2, axis=-1, keepdims=True) + NORM_EPS) * gf_ref[...]


def _combine(dest3, h1, info, gf, yb, tm):
    m = h1.shape[0]
    return pl.pallas_call(
        _combine_body,
        grid=(m // tm,),
        in_specs=[
            pl.BlockSpec((1, 1, 2 * tm), lambda i: (i, 0, 0), memory_space=pltpu.SMEM),
            pl.BlockSpec((tm, D_MODEL), lambda i: (i, 0)),
            pl.BlockSpec((tm, LANES), lambda i: (i, 0)),
            pl.BlockSpec((1, D_MODEL), lambda i: (0, 0)),
            pl.BlockSpec(memory_space=pl.ANY),
        ],
        out_specs=pl.BlockSpec((tm, D_MODEL), lambda i: (i, 0)),
        out_shape=jax.ShapeDtypeStruct((m, D_MODEL), F32),
        scratch_shapes=[pltpu.VMEM((2, tm, D_MODEL), F32), pltpu.SemaphoreType.DMA(())],
        compiler_params=_cparams("arbitrary"),
        name="combine",
    )(dest3, h1, info, gf, yb)


def _tile(m, pref):
    t = pref
    while m % t:
        t //= 2
    return t


def kernel(x, meta_tokens, norm_mix_g, w_in, shift_mu, w0, w_decay_up, a0, w_aaa_up, w_gate_up, k_k, k_a, r_k,
           ln_x_g, ln_x_b, attn_sinks, w_o_attn, w_o_rwkv, w_out, norm_ffn_g, w_grp, b_grp, w_exp, b_exp,
           e_gate, e_up, e_down, norm_final_g):
    assert norm_mix_g.shape[0] == 1, "single-layer trunk"
    b, s, d = x.shape
    assert d == D_MODEL and s % BLOCK == 0
    m = b * s
    row = lambda t: t.reshape(1, -1).astype(F32)
    w_in_b = w_in[0].astype(BF16)
    g_mix = row(norm_mix_g[0])
    rp = dict(mu=row(shift_mu[0]), w0=row(w0[0]), wdec=w_decay_up[0].astype(BF16), a0=row(a0[0]),
              waaa=w_aaa_up[0].astype(BF16), wgate=w_gate_up[0].astype(BF16), kk=row(k_k[0]), ka=row(k_a[0]),
              rk=row(r_k[0]), lng=row(ln_x_g[0]), lnb=row(ln_x_b[0]))

    x0 = jnp.concatenate([jnp.zeros((BLOCK - N_META, d), F32), meta_tokens.astype(F32)], axis=0)
    qkv0, rw0, _ = _proj(x0, g_mix, w_in_b, BLOCK)
    kmeta = qkv0[BLOCK - N_META:, ATT_Q:ATT_Q + ATT_KV]
    vmeta = qkv0[BLOCK - N_META:, ATT_Q + ATT_KV:]
    zero_state = jnp.zeros((RWKV_HEADS, RWKV_HEAD_DIM, RWKV_HEAD_DIM), F32)
    _, s_meta = _rwkv(rw0[None], jnp.zeros((1, RWKV_COLS), F32), zero_state, rp)

    x2d = x.reshape(m, d)
    qkv, rw, gate = _proj(x2d, g_mix, w_in_b, _tile(m, ROW_TILE))
    o_att = _attention(qkv.reshape(b, s, ATT_COLS), kmeta, vmeta, attn_sinks[0].astype(F32))
    yg, _ = _rwkv(rw.reshape(b, s, RWKV_COLS), rw0[BLOCK - 1:], s_meta[0], rp)

    wr = jnp.zeros((d, LANES), F32).at[:, :N_GROUPS].set(w_grp[0]).at[:, N_GROUPS:N_GROUPS + N_EXPERTS].set(w_exp[0])
    br = jnp.zeros((1, LANES), F32).at[0, :N_GROUPS].set(b_grp[0]).at[0, N_GROUPS:N_GROUPS + N_EXPERTS].set(b_exp[0])
    tm = _tile(m, DMA_TILE)
    h1, xn2, info, cnt = _merge(x2d, o_att.reshape(m, ATT_Q), yg.reshape(m, RWKV_DIM), gate,
                                w_o_attn[0].astype(BF16), w_o_rwkv[0].astype(BF16), w_out[0].astype(BF16),
                                row(norm_ffn_g[0]), wr, br, _tile(m, ROW_TILE))

    counts = cnt[0, N_GROUPS:N_GROUPS + N_EXPERTS].astype(jnp.int32)
    padded = ((counts + SLOT_BLOCK - 1) // SLOT_BLOCK) * SLOT_BLOCK
    pends = jnp.cumsum(padded)
    pstarts = pends - padded
    eid = info[:, 0:2].astype(jnp.int32)
    dest = pstarts[eid] + info[:, 4:6].astype(jnp.int32)
    nblk = -(-(m * TOP_K + N_EXPERTS * (SLOT_BLOCK - 1)) // SLOT_BLOCK)
    cap = nblk * SLOT_BLOCK
    block_start = jnp.arange(nblk, dtype=jnp.int32) * SLOT_BLOCK
    block_expert = jnp.minimum(jnp.sum(block_start[:, None] >= pends[None, :], axis=1), N_EXPERTS - 1).astype(jnp.int32)
    dest3 = dest.reshape(m // tm, 1, 2 * tm)

    xs = _dispatch(dest3, xn2, jnp.zeros((cap, d), F32), tm)
    yb = _experts(block_expert, xs, e_gate[0].astype(BF16), e_up[0].astype(BF16), e_down[0].astype(BF16))
    out = _combine(dest3, h1, info, row(norm_final_g), yb, tm)
    return out.reshape(b, s, d)
```

```python
import functools

import numpy as np
import jax
import jax.numpy as jnp
from jax import lax
from jax.experimental import pallas as pl
from jax.experimental.pallas import tpu as pltpu

F32 = jnp.float32
BF16 = jnp.bfloat16

D_MODEL = 1024
N_META = 16
ATT_HEADS = 8
ATT_KV_HEADS = 2
ATT_GROUP = ATT_HEADS // ATT_KV_HEADS
HEAD_DIM = 64
WINDOW = 128
BLOCK = 128
RWKV_HEADS = 8
RWKV_HEAD_DIM = 64
RWKV_DIM = RWKV_HEADS * RWKV_HEAD_DIM
DECAY_LORA = 64
AAA_LORA = 64
GATE_LORA = 128
LN_X_EPS = 64e-5
ATT_Q = ATT_HEADS * HEAD_DIM
ATT_KV = ATT_KV_HEADS * HEAD_DIM
ATT_COLS = ATT_Q + 2 * ATT_KV
RWKV_COLS = 3 * RWKV_DIM + DECAY_LORA + AAA_LORA + GATE_LORA
GATE_COLS = 2 * D_MODEL
IN_COLS = ATT_COLS + RWKV_COLS + GATE_COLS
N_GROUPS = 4
EXPERTS_PER_GROUP = 8
N_EXPERTS = N_GROUPS * EXPERTS_PER_GROUP
TOP_K = 2
D_EXPERT = 256
NORM_EPS = 1e-6

LANES = 128
CHUNK = 64
RWKV_ROWS = 2
SLOT_BLOCK = 256
ROW_TILE = 512
DMA_TILE = 256
DMA_UNROLL = 8
ROUTER_LANE0 = N_GROUPS
VMEM_LIMIT = 56 * 1024 * 1024


def _cparams(*sem):
    return pltpu.CompilerParams(dimension_semantics=sem, vmem_limit_bytes=VMEM_LIMIT)


def _dot(a, b):
    return jnp.dot(a, b, preferred_element_type=F32)


def _dot_nt(a, b):
    return lax.dot_general(a, b, (((1,), (1,)), ((), ())), preferred_element_type=F32)


def _dot_tn(a, b):
    return lax.dot_general(a, b, (((0,), (0,)), ((), ())), preferred_element_type=F32)


def _sigmoid(x):
    return 0.5 * jnp.tanh(0.5 * x) + 0.5


def _proj_body(x_ref, g_ref, w_ref, qkv_ref, rw_ref, gate_ref):
    x = x_ref[...]
    y = x * lax.rsqrt(jnp.mean(x * x, axis=-1, keepdims=True) + NORM_EPS)
    xb = (y * g_ref[...]).astype(BF16)
    qkv_ref[...] = _dot(xb, w_ref[:, :ATT_COLS]).astype(BF16)
    rw_ref[...] = _dot(xb, w_ref[:, ATT_COLS:ATT_COLS + RWKV_COLS])
    gate_ref[...] = _dot(xb, w_ref[:, ATT_COLS + RWKV_COLS:])


def _proj(x2d, g, w_bf16, tm):
    m = x2d.shape[0]
    return pl.pallas_call(
        _proj_body,
        grid=(m // tm,),
        in_specs=[
            pl.BlockSpec((tm, D_MODEL), lambda i: (i, 0)),
            pl.BlockSpec((1, D_MODEL), lambda i: (0, 0)),
            pl.BlockSpec((D_MODEL, IN_COLS), lambda i: (0, 0)),
        ],
        out_specs=[
            pl.BlockSpec((tm, ATT_COLS), lambda i: (i, 0)),
            pl.BlockSpec((tm, RWKV_COLS), lambda i: (i, 0)),
            pl.BlockSpec((tm, GATE_COLS), lambda i: (i, 0)),
        ],
        out_shape=[
            jax.ShapeDtypeStruct((m, ATT_COLS), BF16),
            jax.ShapeDtypeStruct((m, RWKV_COLS), F32),
            jax.ShapeDtypeStruct((m, GATE_COLS), F32),
        ],
        compiler_params=_cparams("parallel"),
        name="proj",
    )(x2d, g, w_bf16)


def _attn_body(sink_ref, q_ref, kc_ref, kp_ref, vc_ref, vp_ref, km_ref, vm_ref, o_ref):
    n = pl.program_id(1)
    rows = ATT_GROUP * BLOCK
    ri = lax.broadcasted_iota(jnp.int32, (rows, BLOCK), 0)
    kj = lax.broadcasted_iota(jnp.int32, (rows, BLOCK), 1)
    d_cur = (ri % BLOCK) - kj
    valid_cur = d_cur >= 0
    valid_prev = jnp.logical_and(d_cur < 0, n > 0)
    f_cur = d_cur.astype(F32)
    f_prev = f_cur + float(BLOCK)
    rm = lax.broadcasted_iota(jnp.int32, (rows, N_META), 0)
    mi = lax.broadcasted_iota(jnp.int32, (rows, N_META), 1)
    f_meta = jnp.minimum(N_META + n * BLOCK + (rm % BLOCK) - mi, WINDOW).astype(F32)
    head_in_group = lax.broadcasted_iota(jnp.int32, (rows, 1), 0) // BLOCK
    scale = HEAD_DIM ** -0.5
    neg_inf = jnp.float32(-jnp.inf)
    groups = range(ATT_KV_HEADS)
    cols = [slice(g * HEAD_DIM, (g + 1) * HEAD_DIM) for g in groups]

    slope, sink = [], []
    for g in groups:
        sl_g = jnp.zeros((rows, 1), F32)
        sk_g = jnp.zeros((rows, 1), F32)
        for j in range(ATT_GROUP):
            h = g * ATT_GROUP + j
            sl_g = jnp.where(head_in_group == j, float(np.float32(2.0 ** (-8.0 * (h + 1) / ATT_HEADS))), sl_g)
            sk_g = jnp.where(head_in_group == j, sink_ref[h], sk_g)
        slope.append(sl_g)
        sink.append(sk_g)

    q4 = [jnp.concatenate([q_ref[:, (g * ATT_GROUP + j) * HEAD_DIM:(g * ATT_GROUP + j + 1) * HEAD_DIM]
                           for j in range(ATT_GROUP)], axis=0) for g in groups]
    s_c = [_dot_nt(q4[g], kc_ref[:, cols[g]]) for g in groups]
    s_p = [_dot_nt(q4[g], kp_ref[:, cols[g]]) for g in groups]
    s_m = [_dot_nt(q4[g], km_ref[:, cols[g]]) for g in groups]
    l_c = [jnp.where(valid_cur, s_c[g] * scale - slope[g] * f_cur, neg_inf) for g in groups]
    l_p = [jnp.where(valid_prev, s_p[g] * scale - slope[g] * f_prev, neg_inf) for g in groups]
    l_m = [s_m[g] * scale - slope[g] * f_meta for g in groups]
    m = [jnp.maximum(jnp.maximum(jnp.max(jnp.maximum(l_c[g], l_p[g]), axis=-1, keepdims=True),
                                 jnp.max(l_m[g], axis=-1, keepdims=True)), sink[g]) for g in groups]
    p_c = [jnp.exp(l_c[g] - m[g]) for g in groups]
    p_p = [jnp.exp(l_p[g] - m[g]) for g in groups]
    p_m = [jnp.exp(l_m[g] - m[g]) for g in groups]
    inv = [1.0 / (jnp.sum(p_c[g] + p_p[g], axis=-1, keepdims=True) + jnp.sum(p_m[g], axis=-1, keepdims=True)
                  + jnp.exp(sink[g] - m[g])) for g in groups]
    o4 = [(_dot((p_c[g] * inv[g]).astype(BF16), vc_ref[:, cols[g]])
           + _dot((p_p[g] * inv[g]).astype(BF16), vp_ref[:, cols[g]])
           + _dot((p_m[g] * inv[g]).astype(BF16), vm_ref[:, cols[g]])).astype(BF16) for g in groups]
    o_ref[...] = jnp.concatenate([o4[g][j * BLOCK:(j + 1) * BLOCK] for g in groups for j in range(ATT_GROUP)], axis=1)


def _attention(qkv, kmeta, vmeta, sinks):
    b, s, _ = qkv.shape
    nb = s // BLOCK
    kcol, vcol = ATT_Q // ATT_KV, ATT_Q // ATT_KV + 1
    prev = lambda n: jnp.maximum(n - 1, 0)
    return pl.pallas_call(
        _attn_body,
        grid=(b, nb),
        in_specs=[
            pl.BlockSpec(memory_space=pltpu.SMEM),
            pl.BlockSpec((None, BLOCK, ATT_Q), lambda i, n: (i, n, 0)),
            pl.BlockSpec((None, BLOCK, ATT_KV), lambda i, n: (i, n, kcol)),
            pl.BlockSpec((None, BLOCK, ATT_KV), lambda i, n: (i, prev(n), kcol)),
            pl.BlockSpec((None, BLOCK, ATT_KV), lambda i, n: (i, n, vcol)),
            pl.BlockSpec((None, BLOCK, ATT_KV), lambda i, n: (i, prev(n), vcol)),
            pl.BlockSpec((N_META, ATT_KV), lambda i, n: (0, 0)),
            pl.BlockSpec((N_META, ATT_KV), lambda i, n: (0, 0)),
        ],
        out_specs=pl.BlockSpec((None, BLOCK, ATT_Q), lambda i, n: (i, n, 0)),
        out_shape=jax.ShapeDtypeStruct((b, s, ATT_Q), BF16),
        compiler_params=_cparams("parallel", "parallel"),
        name="attn",
    )(sinks, qkv, qkv, qkv, qkv, qkv, kmeta, vmeta)


def _rwkv_prep(rw_ref, prev_ref, shift_ref, rowp_ref, lora_ref, dst, wr, b):
    ar_s, bt_s, kt_s, bke_s, vb_s, bonus_s, g_s, plast_s = dst
    L, N = CHUNK, RWKV_HEAD_DIM
    w0, a0, kk, ka, rk = (rowp_ref[i:i + 1, :] for i in range(5))
    wdec = lora_ref[:DECAY_LORA, :]
    waaa = lora_ref[DECAY_LORA:DECAY_LORA + AAA_LORA, :]
    wgate = lora_ref[DECAY_LORA + AAA_LORA:, :]
    c = rw_ref[b]
    row = lax.broadcasted_iota(jnp.int32, (L, 1), 0)
    cp = jnp.where(row == 0, prev_ref[b], pltpu.roll(c, 1, 0))
    prev_ref[b] = c[L - 1:L, :]
    x = c + shift_ref[0:1, :] * (cp - c)
    r = x[:, :RWKV_DIM]
    k = x[:, RWKV_DIM:2 * RWKV_DIM]
    v = x[:, 2 * RWKV_DIM:3 * RWKV_DIM]
    o = 3 * RWKV_DIM
    xw = x[:, o:o + DECAY_LORA]
    xa = x[:, o + DECAY_LORA:o + DECAY_LORA + AAA_LORA]
    xg = x[:, o + DECAY_LORA + AAA_LORA:]

    z = -(w0 + _dot(jnp.tanh(xw).astype(BF16), wdec))
    a = _sigmoid(a0 + _dot(xa.astype(BF16), waaa))
    g_s[wr, b] = _dot(_sigmoid(xg).astype(BF16), wgate)
    softplus = jnp.maximum(z, 0.0) + jnp.log(1.0 + jnp.exp(-jnp.abs(z)))
    lw = -jnp.exp(-softplus - 0.5)
    kkr = k * kk
    k2 = k * (1.0 + (a - 1.0) * ka)

    cum = lw
    shift = 1
    while shift < L:
        cum = cum + jnp.where(row >= shift, pltpu.roll(cum, shift, 0), 0.0)
        shift *= 2
    cum_last = cum[L - 1:L, :]
    p_inc = jnp.exp(cum)
    p_inv = jnp.exp(-cum)
    p_prev = jnp.exp(cum - lw)
    p_end = jnp.exp(cum_last - cum)
    plast_s[wr, b] = jnp.exp(cum_last)

    for h in range(RWKV_HEADS):
        sl = slice(h * N, (h + 1) * N)
        i = b * RWKV_HEADS + h
        kkh = kkr[:, sl]
        kkn = kkh / jnp.maximum(jnp.sqrt(jnp.sum(kkh * kkh, axis=-1, keepdims=True)), 1e-12)
        bv = kkn * a[:, sl]
        rh, k2h, vh = r[:, sl], k2[:, sl], v[:, sl]
        ar_s[wr, i] = jnp.concatenate([(-kkn * p_prev[:, sl]).astype(BF16), (rh * p_inc[:, sl]).astype(BF16)], axis=0)
        bt_s[wr, i] = (bv * p_inv[:, sl]).astype(BF16)
        kt_s[wr, i] = (k2h * p_inv[:, sl]).astype(BF16)
        bke_s[wr, i] = jnp.concatenate([(bv * p_end[:, sl]).astype(BF16), (k2h * p_end[:, sl]).astype(BF16)], axis=0)
        vb_s[wr, i] = vh.astype(BF16)
        bonus_s[wr, i] = jnp.sum(rh * k2h * rk[:, sl], axis=-1, keepdims=True) * vh


def _rwkv_chain(src, rd, state_ref, rowp_ref, yg_ref):
    ar_s, bt_s, kt_s, bke_s, vb_s, bonus_s, g_s, plast_s = src
    L, N = CHUNK, RWKV_HEAD_DIM
    rows = yg_ref.shape[0]
    heads = range(rows * RWKV_HEADS)
    sls = [slice((i % RWKV_HEADS) * N, (i % RWKV_HEADS + 1) * N) for i in heads]
    lng, lnb = rowp_ref[5:6, :], rowp_ref[6:7, :]
    ti = lax.broadcasted_iota(jnp.int32, (L, L), 0)
    tj = lax.broadcasted_iota(jnp.int32, (L, L), 1)
    strict = ti > tj
    incl = ti >= tj
    eye = (ti == tj).astype(F32)
    mask_akrk = jnp.concatenate([strict, incl], axis=0)

    ar = [ar_s[rd, h] for h in heads]
    vb = [vb_s[rd, h] for h in heads]
    s_old = [state_ref[h] for h in heads]
    sob = [x.astype(BF16) for x in s_old]
    sb = [_dot_nt(ar[h], bt_s[rd, h]) for h in heads]
    sk = [_dot_nt(ar[h], kt_s[rd, h]) for h in heads]
    ars = [_dot_nt(ar[h], sob[h]) for h in heads]
    a_ab = [jnp.where(strict, sb[h][:L], 0.0) for h in heads]
    a_rb = [jnp.where(incl, sb[h][L:], 0.0).astype(BF16) for h in heads]
    akrk = [jnp.where(mask_akrk, sk[h], 0.0).astype(BF16) for h in heads]
    akv = [_dot(akrk[h], vb[h]) for h in heads]

    t_inv = [eye + a_ab[h] for h in heads]
    ab = [x.astype(BF16) for x in a_ab]
    pk = [_dot(ab[h], ab[h]) for h in heads]
    for step in range(5):
        pkb = [x.astype(BF16) for x in pk]
        t_inv = [t_inv[h] + _dot(t_inv[h].astype(BF16), pkb[h]) for h in heads]
        if step < 4:
            pk = [_dot(pkb[h], pkb[h]) for h in heads]

    z = [(ars[h][:L] + akv[h][:L]).astype(BF16) for h in heads]
    ub = [_dot(t_inv[h].astype(BF16), z[h]).astype(BF16) for h in heads]
    y = [ars[h][L:] + akv[h][L:] + _dot(a_rb[h], ub[h]) for h in heads]
    upd = [_dot_tn(jnp.concatenate([ub[h], vb[h]], axis=0), bke_s[rd, h]) for h in heads]
    for h in heads:
        state_ref[h] = s_old[h] * plast_s[rd, h // RWKV_HEADS][:, sls[h]] + upd[h]

    for b in range(rows):
        g = g_s[rd, b]
        outs = []
        for h in range(b * RWKV_HEADS, (b + 1) * RWKV_HEADS):
            mean = jnp.mean(y[h], axis=-1, keepdims=True)
            yc = y[h] - mean
            var = jnp.mean(yc * yc, axis=-1, keepdims=True)
            yn = yc * lax.rsqrt(var + LN_X_EPS) * lng[:, sls[h]] + lnb[:, sls[h]]
            outs.append((yn + bonus_s[rd, h]) * g[:, sls[h]])
        yg_ref[b] = jnp.concatenate(outs, axis=1).astype(BF16)


def _rwkv_body(rw_ref, shift_ref, s0_ref, rowp_ref, lora_ref, yg_ref, sfin_ref, state_ref, prev_ref, *slots):
    ci = pl.program_id(1)
    rows = rw_ref.shape[0]

    @pl.when(ci == 0)
    def _():
        for b in range(rows):
            state_ref[b * RWKV_HEADS:(b + 1) * RWKV_HEADS] = s0_ref[...]
            prev_ref[b] = shift_ref[1:2, :]
        for ref in slots[:-1]:
            ref[0] = jnp.zeros(ref.shape[1:], ref.dtype)
        slots[-1][0] = jnp.ones(slots[-1].shape[1:], F32)

    def step(rd, wr):
        for b in range(rows):
            _rwkv_prep(rw_ref, prev_ref, shift_ref, rowp_ref, lora_ref, slots, wr, b)
        _rwkv_chain(slots, rd, state_ref, rowp_ref, yg_ref)

    @pl.when(ci % 2 == 0)
    def _():
        step(0, 1)

    @pl.when(ci % 2 == 1)
    def _():
        step(1, 0)

    @pl.when(ci == pl.num_programs(1) - 1)
    def _():
        sfin_ref[...] = state_ref[...]


def _rwkv(rw, cprev0, s0, p):
    b, s, _ = rw.shape
    nc = s // CHUNK
    L, N, H = CHUNK, RWKV_HEAD_DIM, RWKV_HEADS
    rows = RWKV_ROWS if b % RWKV_ROWS == 0 else 1
    shift = jnp.concatenate([p["mu"], cprev0], axis=0)
    full = lambda shape: pl.BlockSpec(shape, lambda i, c: (0,) * len(shape))
    yg, sfin = pl.pallas_call(
        _rwkv_body,
        grid=(b // rows, nc + 1),
        in_specs=[
            pl.BlockSpec((rows, CHUNK, RWKV_COLS), lambda i, c: (i, jnp.minimum(c, nc - 1), 0)),
            full(shift.shape),
            full((H, N, N)),
            full(p["rows"].shape),
            full(p["lora"].shape),
        ],
        out_specs=[
            pl.BlockSpec((rows, CHUNK, RWKV_DIM), lambda i, c: (i, jnp.maximum(c - 1, 0), 0)),
            pl.BlockSpec((rows * H, N, N), lambda i, c: (i, 0, 0)),
        ],
        out_shape=[
            jax.ShapeDtypeStruct((b, s, RWKV_DIM), BF16),
            jax.ShapeDtypeStruct((b * H, N, N), F32),
        ],
        scratch_shapes=[
            pltpu.VMEM((rows * H, N, N), F32),
            pltpu.VMEM((rows, 1, RWKV_COLS), F32),
            pltpu.VMEM((2, rows * H, 2 * L, N), BF16),
            pltpu.VMEM((2, rows * H, L, N), BF16),
            pltpu.VMEM((2, rows * H, L, N), BF16),
            pltpu.VMEM((2, rows * H, 2 * L, N), BF16),
            pltpu.VMEM((2, rows * H, L, N), BF16),
            pltpu.VMEM((2, rows * H, L, N), F32),
            pltpu.VMEM((2, rows, L, RWKV_DIM), F32),
            pltpu.VMEM((2, rows, 1, RWKV_DIM), F32),
        ],
        compiler_params=_cparams("arbitrary", "arbitrary"),
        name="rwkv",
    )(rw, shift, s0, p["rows"], p["lora"])
    return yg, sfin.reshape(b, H, N, N)


def _merge_body(x_ref, oa_ref, yg_ref, gate_ref, woa_ref, wor_ref, wout_ref, g2_ref, wrh_ref, wrl_ref, br_ref,
                h1_ref, xn_ref, info_ref, cnt_ref, base_ref):
    tm = x_ref.shape[0]

    @pl.when(pl.program_id(0) == 0)
    def _():
        base_ref[...] = jnp.zeros_like(base_ref)

    att = _dot(oa_ref[...], woa_ref[...])
    rwk = _dot(yg_ref[...], wor_ref[...])
    gate = gate_ref[...]
    merged = _sigmoid(gate[:, :D_MODEL]) * att + _sigmoid(gate[:, D_MODEL:]) * rwk
    h1 = x_ref[...] + _dot(merged.astype(BF16), wout_ref[...])
    h1_ref[...] = h1
    xn = h1 * lax.rsqrt(jnp.mean(h1 * h1, axis=-1, keepdims=True) + NORM_EPS) * g2_ref[...]
    xn_ref[...] = xn

    xn_hi = xn.astype(BF16)
    xn_lo = (xn - xn_hi.astype(F32)).astype(BF16)
    logits = (_dot(xn_hi, wrh_ref[...]) + _dot(xn_lo, wrh_ref[...]) + _dot(xn_hi, wrl_ref[...])) + br_ref[...]
    lane = lax.broadcasted_iota(jnp.int32, (tm, LANES), 1)
    neg_inf = jnp.float32(-jnp.inf)
    big = jnp.int32(LANES)
    gl = jnp.where(lane < N_GROUPS, logits, neg_inf)
    gmax = jnp.max(gl, axis=-1, keepdims=True)
    gsel = jnp.min(jnp.where(gl == gmax, lane, big), axis=-1, keepdims=True)
    p_sel = 1.0 / jnp.sum(jnp.exp(gl - gmax), axis=-1, keepdims=True)
    lo = ROUTER_LANE0 + gsel * EXPERTS_PER_GROUP
    el = jnp.where(jnp.logical_and(lane >= lo, lane < lo + EXPERTS_PER_GROUP), logits, neg_inf)
    v1 = jnp.max(el, axis=-1, keepdims=True)
    i1 = jnp.min(jnp.where(el == v1, lane, big), axis=-1, keepdims=True)
    el2 = jnp.where(lane == i1, neg_inf, el)
    v2 = jnp.max(el2, axis=-1, keepdims=True)
    i2 = jnp.min(jnp.where(el2 == v2, lane, big), axis=-1, keepdims=True)
    e2 = jnp.exp(v2 - v1)
    w1 = p_sel / (1.0 + e2)
    w2 = p_sel * e2 / (1.0 + e2)

    oh1 = (lane == i1).astype(F32)
    oh2 = (lane == i2).astype(F32)
    both = oh1 + oh2
    ti = lax.broadcasted_iota(jnp.int32, (tm, tm), 0)
    tj = lax.broadcasted_iota(jnp.int32, (tm, tm), 1)
    before = _dot((ti > tj).astype(BF16), both.astype(BF16)) + base_ref[...]
    rank1 = jnp.sum(before * oh1, axis=-1, keepdims=True)
    rank2 = jnp.sum(before * oh2, axis=-1, keepdims=True)
    base = base_ref[...] + jnp.sum(both, axis=0, keepdims=True)
    base_ref[...] = base
    cnt_ref[...] = base

    info = jnp.where(lane == 0, (i1 - ROUTER_LANE0).astype(F32), 0.0)
    info = jnp.where(lane == 1, (i2 - ROUTER_LANE0).astype(F32), info)
    info = jnp.where(lane == 2, w1, info)
    info = jnp.where(lane == 3, w2, info)
    info = jnp.where(lane == 4, rank1, info)
    info = jnp.where(lane == 5, rank2, info)
    info_ref[...] = info


def _merge(x2d, oa, yg, gate, woa, wor, wout, g2, wr, br, tm):
    wr_hi = wr.astype(BF16)
    wr_lo = (wr - wr_hi.astype(F32)).astype(BF16)
    m = x2d.shape[0]
    tile = lambda n: pl.BlockSpec((tm, n), lambda i: (i, 0))
    full = lambda a, b: pl.BlockSpec((a, b), lambda i: (0, 0))
    return pl.pallas_call(
        _merge_body,
        grid=(m // tm,),
        in_specs=[
            tile(D_MODEL), tile(ATT_Q), tile(RWKV_DIM), tile(GATE_COLS),
            full(ATT_Q, D_MODEL), full(RWKV_DIM, D_MODEL), full(D_MODEL, D_MODEL),
            full(1, D_MODEL), full(D_MODEL, LANES), full(D_MODEL, LANES), full(1, LANES),
        ],
        out_specs=[tile(D_MODEL), tile(D_MODEL), tile(LANES), full(1, LANES)],
        out_shape=[
            jax.ShapeDtypeStruct((m, D_MODEL), F32),
            jax.ShapeDtypeStruct((m, D_MODEL), F32),
            jax.ShapeDtypeStruct((m, LANES), F32),
            jax.ShapeDtypeStruct((1, LANES), F32),
        ],
        scratch_shapes=[pltpu.VMEM((1, LANES), F32)],
        compiler_params=_cparams("arbitrary"),
        name="merge",
    )(x2d, oa, yg, gate, woa, wor, wout, g2, wr_hi, wr_lo, br)


def _dispatch_body(dest_ref, xn_ref, xs_in_ref, xs_ref, sem):
    del xs_in_ref
    tm = xn_ref.shape[0]

    def row_copy(r, d):
        return pltpu.make_async_copy(xn_ref.at[pl.ds(r, 1)], xs_ref.at[pl.ds(d, 1)], sem)

    def start(r, carry):
        row_copy(r, dest_ref[0, 0, 2 * r]).start()
        row_copy(r, dest_ref[0, 0, 2 * r + 1]).start()
        return carry

    def wait(r, carry):
        row_copy(0, 0).wait()
        row_copy(0, 0).wait()
        return carry

    lax.fori_loop(0, tm, start, 0, unroll=DMA_UNROLL)
    lax.fori_loop(0, tm, wait, 0, unroll=DMA_UNROLL)


def _dispatch(dest3, xn, xs_zero):
    tm = dest3.shape[2] // TOP_K
    return pl.pallas_call(
        _dispatch_body,
        grid=(dest3.shape[0],),
        in_specs=[
            pl.BlockSpec((1, 1, dest3.shape[2]), lambda i: (i, 0, 0), memory_space=pltpu.SMEM),
            pl.BlockSpec((tm, D_MODEL), lambda i: (i, 0)),
            pl.BlockSpec(memory_space=pl.ANY),
        ],
        out_specs=pl.BlockSpec(memory_space=pl.ANY),
        out_shape=jax.ShapeDtypeStruct(xs_zero.shape, xs_zero.dtype),
        scratch_shapes=[pltpu.SemaphoreType.DMA(())],
        input_output_aliases={2: 0},
        compiler_params=_cparams("arbitrary"),
        name="dispatch",
    )(dest3, xn, xs_zero)


def _expert_body(be_ref, xs_ref, wg_ref, wu_ref, wd_ref, yb_ref):
    del be_ref
    xb = xs_ref[...].astype(BF16)
    gt = _dot(xb, wg_ref[...])
    up = _dot(xb, wu_ref[...])
    hmid = gt * _sigmoid(gt) * up
    yb_ref[...] = _dot(hmid.astype(BF16), wd_ref[...])


def _experts(block_expert, xs, wg, wu, wd):
    cap = xs.shape[0]
    nblk = cap // SLOT_BLOCK
    return pl.pallas_call(
        _expert_body,
        grid_spec=pltpu.PrefetchScalarGridSpec(
            num_scalar_prefetch=1,
            grid=(nblk,),
            in_specs=[
                pl.BlockSpec((SLOT_BLOCK, D_MODEL), lambda i, be: (i, 0)),
                pl.BlockSpec((None, D_MODEL, D_EXPERT), lambda i, be: (be[i], 0, 0)),
                pl.BlockSpec((None, D_MODEL, D_EXPERT), lambda i, be: (be[i], 0, 0)),
                pl.BlockSpec((None, D_EXPERT, D_MODEL), lambda i, be: (be[i], 0, 0)),
            ],
            out_specs=pl.BlockSpec((SLOT_BLOCK, D_MODEL), lambda i, be: (i, 0)),
        ),
        out_shape=jax.ShapeDtypeStruct((cap, D_MODEL), F32),
        compiler_params=_cparams("arbitrary"),
        name="experts",
    )(block_expert, xs, wg, wu, wd)


def _combine_body(dcur_ref, dnext_ref, h1_ref, info_ref, gf_ref, yb_ref, out_ref, buf_ref, sem):
    i = pl.program_id(0)
    tm = h1_ref.shape[0]

    def row_copy(d, slot, k, r):
        return pltpu.make_async_copy(yb_ref.at[pl.ds(d, 1)], buf_ref.at[slot, k, pl.ds(r, 1)], sem.at[slot])

    def issue(dref, slot):
        def start(r, carry):
            row_copy(dref[0, 0, 2 * r], slot, 0, r).start()
            row_copy(dref[0, 0, 2 * r + 1], slot, 1, r).start()
            return carry
        lax.fori_loop(0, tm, start, 0, unroll=DMA_UNROLL)

    def step(slot):
        @pl.when(i == 0)
        def _():
            issue(dcur_ref, slot)

        @pl.when(i + 1 < pl.num_programs(0))
        def _():
            issue(dnext_ref, 1 - slot)

        def wait(r, carry):
            row_copy(0, slot, 0, 0).wait()
            row_copy(0, slot, 1, 0).wait()
            return carry
        lax.fori_loop(0, tm, wait, 0, unroll=DMA_UNROLL)
        info = info_ref[...]
        y = buf_ref[slot, 0] * info[:, 2:3] + buf_ref[slot, 1] * info[:, 3:4]
        h2 = h1_ref[...] + y
        out_ref[...] = h2 * lax.rsqrt(jnp.mean(h2 * h2, axis=-1, keepdims=True) + NORM_EPS) * gf_ref[...]

    @pl.when(i % 2 == 0)
    def _():
        step(0)

    @pl.when(i % 2 == 1)
    def _():
        step(1)


def _combine(dest3, h1, info, gf, yb):
    m = h1.shape[0]
    nt = dest3.shape[0]
    tm = m // nt
    dspec = lambda f: pl.BlockSpec((1, 1, dest3.shape[2]), f, memory_space=pltpu.SMEM)
    return pl.pallas_call(
        _combine_body,
        grid=(nt,),
        in_specs=[
            dspec(lambda i: (i, 0, 0)),
            dspec(lambda i: (jnp.minimum(i + 1, nt - 1), 0, 0)),
            pl.BlockSpec((tm, D_MODEL), lambda i: (i, 0)),
            pl.BlockSpec((tm, LANES), lambda i: (i, 0)),
            pl.BlockSpec((1, D_MODEL), lambda i: (0, 0)),
            pl.BlockSpec(memory_space=pl.ANY),
        ],
        out_specs=pl.BlockSpec((tm, D_MODEL), lambda i: (i, 0)),
        out_shape=jax.ShapeDtypeStruct((m, D_MODEL), F32),
        scratch_shapes=[pltpu.VMEM((2, TOP_K, tm, D_MODEL), F32), pltpu.SemaphoreType.DMA((2,))],
        compiler_params=_cparams("arbitrary"),
        name="combine",
    )(dest3, dest3, h1, info, gf, yb)


def _tile(m, pref):
    t = pref
    while m % t:
        t //= 2
    return t


def kernel(x, meta_tokens, norm_mix_g, w_in, shift_mu, w0, w_decay_up, a0, w_aaa_up, w_gate_up, k_k, k_a, r_k,
           ln_x_g, ln_x_b, attn_sinks, w_o_attn, w_o_rwkv, w_out, norm_ffn_g, w_grp, b_grp, w_exp, b_exp,
           e_gate, e_up, e_down, norm_final_g):
    assert norm_mix_g.shape[0] == 1, "single-layer trunk"
    b, s, d = x.shape
    assert d == D_MODEL and s % BLOCK == 0
    m = b * s
    row = lambda t: t.reshape(1, -1).astype(F32)
    w_in_b = w_in[0].astype(BF16)
    g_mix = row(norm_mix_g[0])
    rp = dict(mu=row(shift_mu[0]),
              rows=jnp.concatenate([row(t[0]) for t in (w0, a0, k_k, k_a, r_k, ln_x_g, ln_x_b)]
                                   + [jnp.zeros((1, RWKV_DIM), F32)], axis=0),
              lora=jnp.concatenate([w_decay_up[0], w_aaa_up[0], w_gate_up[0]], axis=0).astype(BF16))

    x0 = jnp.concatenate([jnp.zeros((BLOCK - N_META, d), F32), meta_tokens.astype(F32)], axis=0)
    qkv0, rw0, _ = _proj(x0, g_mix, w_in_b, BLOCK)
    kmeta = qkv0[BLOCK - N_META:, ATT_Q:ATT_Q + ATT_KV]
    vmeta = qkv0[BLOCK - N_META:, ATT_Q + ATT_KV:]
    zero_state = jnp.zeros((RWKV_HEADS, RWKV_HEAD_DIM, RWKV_HEAD_DIM), F32)
    _, s_meta = _rwkv(rw0[None], jnp.zeros((1, RWKV_COLS), F32), zero_state, rp)

    x2d = x.reshape(m, d)
    qkv, rw, gate = _proj(x2d, g_mix, w_in_b, _tile(m, ROW_TILE))
    o_att = _attention(qkv.reshape(b, s, ATT_COLS), kmeta, vmeta, attn_sinks[0].astype(F32))
    yg, _ = _rwkv(rw.reshape(b, s, RWKV_COLS), rw0[BLOCK - 1:], s_meta[0], rp)

    wr = jnp.zeros((d, LANES), F32).at[:, :N_GROUPS].set(w_grp[0]).at[:, N_GROUPS:N_GROUPS + N_EXPERTS].set(w_exp[0])
    br = jnp.zeros((1, LANES), F32).at[0, :N_GROUPS].set(b_grp[0]).at[0, N_GROUPS:N_GROUPS + N_EXPERTS].set(b_exp[0])
    tm = _tile(m, DMA_TILE)
    h1, xn2, info, cnt = _merge(x2d, o_att.reshape(m, ATT_Q), yg.reshape(m, RWKV_DIM), gate,
                                w_o_attn[0].astype(BF16), w_o_rwkv[0].astype(BF16), w_out[0].astype(BF16),
                                row(norm_ffn_g[0]), wr, br, _tile(m, ROW_TILE))

    counts = cnt[0, N_GROUPS:N_GROUPS + N_EXPERTS].astype(jnp.int32)
    padded = ((counts + SLOT_BLOCK - 1) // SLOT_BLOCK) * SLOT_BLOCK
    pends = jnp.cumsum(padded)
    pstarts = pends - padded
    eid = info[:, 0:2].astype(jnp.int32)
    dest = pstarts[eid] + info[:, 4:6].astype(jnp.int32)
    nblk = -(-(m * TOP_K + N_EXPERTS * (SLOT_BLOCK - 1)) // SLOT_BLOCK)
    cap = nblk * SLOT_BLOCK
    block_start = jnp.arange(nblk, dtype=jnp.int32) * SLOT_BLOCK
    block_expert = jnp.minimum(jnp.sum(block_start[:, None] >= pends[None, :], axis=1), N_EXPERTS - 1).astype(jnp.int32)
    dest3 = dest.reshape(m // tm, 1, 2 * tm)

    xs = _dispatch(dest3, xn2, jnp.zeros((cap, d), F32))
    yb = _experts(block_expert, xs, e_gate[0].astype(BF16), e_up[0].astype(BF16), e_down[0].astype(BF16))
    out = _combine(dest3, h1, info, row(norm_final_g), yb)
    return out.reshape(b, s, d)
```

```python
import functools

import numpy as np
import jax
import jax.numpy as jnp
from jax import lax
from jax.experimental import pallas as pl
from jax.experimental.pallas import tpu as pltpu

F32 = jnp.float32
BF16 = jnp.bfloat16

D_MODEL = 1024
N_META = 16
ATT_HEADS = 8
ATT_KV_HEADS = 2
ATT_GROUP = ATT_HEADS // ATT_KV_HEADS
HEAD_DIM = 64
WINDOW = 128
BLOCK = 128
RWKV_HEADS = 8
RWKV_HEAD_DIM = 64
RWKV_DIM = RWKV_HEADS * RWKV_HEAD_DIM
DECAY_LORA = 64
AAA_LORA = 64
GATE_LORA = 128
LN_X_EPS = 64e-5
ATT_Q = ATT_HEADS * HEAD_DIM
ATT_KV = ATT_KV_HEADS * HEAD_DIM
ATT_COLS = ATT_Q + 2 * ATT_KV
RWKV_COLS = 3 * RWKV_DIM + DECAY_LORA + AAA_LORA + GATE_LORA
GATE_COLS = 2 * D_MODEL
IN_COLS = ATT_COLS + RWKV_COLS + GATE_COLS
N_GROUPS = 4
EXPERTS_PER_GROUP = 8
N_EXPERTS = N_GROUPS * EXPERTS_PER_GROUP
TOP_K = 2
D_EXPERT = 256
NORM_EPS = 1e-6

LANES = 128
CHUNK = 64
RWKV_ROWS = 4
SLOT_BLOCK = 256
ROW_TILE = 512
DMA_TILE = 256
DMA_UNROLL = 8
ROUTER_LANE0 = N_GROUPS
INFO_COLS = 8
VMEM_LIMIT = 56 * 1024 * 1024


def _cparams(*sem):
    return pltpu.CompilerParams(dimension_semantics=sem, vmem_limit_bytes=VMEM_LIMIT)


def _dot(a, b):
    return jnp.dot(a, b, preferred_element_type=F32)


def _dot_nt(a, b):
    return lax.dot_general(a, b, (((1,), (1,)), ((), ())), preferred_element_type=F32)


def _dot_tn(a, b):
    return lax.dot_general(a, b, (((0,), (0,)), ((), ())), preferred_element_type=F32)


def _sigmoid(x):
    return 0.5 * jnp.tanh(0.5 * x) + 0.5


def _proj_body(x_ref, g_ref, w_ref, qkv_ref, rw_ref, gate_ref):
    x = x_ref[...]
    y = x * lax.rsqrt(jnp.mean(x * x, axis=-1, keepdims=True) + NORM_EPS)
    xb = (y * g_ref[...]).astype(BF16)
    qkv_ref[...] = _dot(xb, w_ref[:, :ATT_COLS]).astype(BF16)
    rw_ref[...] = _dot(xb, w_ref[:, ATT_COLS:ATT_COLS + RWKV_COLS])
    gate_ref[...] = _dot(xb, w_ref[:, ATT_COLS + RWKV_COLS:]).astype(BF16)


def _proj(x2d, g, w_bf16, tm):
    m = x2d.shape[0]
    return pl.pallas_call(
        _proj_body,
        grid=(m // tm,),
        in_specs=[
            pl.BlockSpec((tm, D_MODEL), lambda i: (i, 0)),
            pl.BlockSpec((1, D_MODEL), lambda i: (0, 0)),
            pl.BlockSpec((D_MODEL, IN_COLS), lambda i: (0, 0)),
        ],
        out_specs=[
            pl.BlockSpec((tm, ATT_COLS), lambda i: (i, 0)),
            pl.BlockSpec((tm, RWKV_COLS), lambda i: (i, 0)),
            pl.BlockSpec((tm, GATE_COLS), lambda i: (i, 0)),
        ],
        out_shape=[
            jax.ShapeDtypeStruct((m, ATT_COLS), BF16),
            jax.ShapeDtypeStruct((m, RWKV_COLS), F32),
            jax.ShapeDtypeStruct((m, GATE_COLS), BF16),
        ],
        compiler_params=_cparams("parallel"),
        name="proj",
    )(x2d, g, w_bf16)


def _attn_body(sink_ref, q_ref, kc_ref, kp_ref, vc_ref, vp_ref, km_ref, vm_ref, o_ref):
    n = pl.program_id(1)
    rows = ATT_GROUP * BLOCK
    ri = lax.broadcasted_iota(jnp.int32, (rows, BLOCK), 0)
    kj = lax.broadcasted_iota(jnp.int32, (rows, BLOCK), 1)
    d_cur = (ri % BLOCK) - kj
    valid_cur = d_cur >= 0
    valid_prev = jnp.logical_and(d_cur < 0, n > 0)
    f_cur = d_cur.astype(F32)
    f_prev = f_cur + float(BLOCK)
    rm = lax.broadcasted_iota(jnp.int32, (rows, N_META), 0)
    mi = lax.broadcasted_iota(jnp.int32, (rows, N_META), 1)
    f_meta = jnp.minimum(N_META + n * BLOCK + (rm % BLOCK) - mi, WINDOW).astype(F32)
    head_in_group = lax.broadcasted_iota(jnp.int32, (rows, 1), 0) // BLOCK
    scale = HEAD_DIM ** -0.5
    neg_inf = jnp.float32(-jnp.inf)
    groups = range(ATT_KV_HEADS)
    cols = [slice(g * HEAD_DIM, (g + 1) * HEAD_DIM) for g in groups]

    slope, sink = [], []
    for g in groups:
        sl_g = jnp.zeros((rows, 1), F32)
        sk_g = jnp.zeros((rows, 1), F32)
        for j in range(ATT_GROUP):
            h = g * ATT_GROUP + j
            sl_g = jnp.where(head_in_group == j, float(np.float32(2.0 ** (-8.0 * (h + 1) / ATT_HEADS))), sl_g)
            sk_g = jnp.where(head_in_group == j, sink_ref[h], sk_g)
        slope.append(sl_g)
        sink.append(sk_g)

    q4 = [jnp.concatenate([q_ref[:, (g * ATT_GROUP + j) * HEAD_DIM:(g * ATT_GROUP + j + 1) * HEAD_DIM]
                           for j in range(ATT_GROUP)], axis=0) for g in groups]
    s_c = [_dot_nt(q4[g], kc_ref[:, cols[g]]) for g in groups]
    s_p = [_dot_nt(q4[g], kp_ref[:, cols[g]]) for g in groups]
    s_m = [_dot_nt(q4[g], km_ref[:, cols[g]]) for g in groups]
    l_c = [jnp.where(valid_cur, s_c[g] * scale - slope[g] * f_cur, neg_inf) for g in groups]
    l_p = [jnp.where(valid_prev, s_p[g] * scale - slope[g] * f_prev, neg_inf) for g in groups]
    l_m = [s_m[g] * scale - slope[g] * f_meta for g in groups]
    m = [jnp.maximum(jnp.maximum(jnp.max(jnp.maximum(l_c[g], l_p[g]), axis=-1, keepdims=True),
                                 jnp.max(l_m[g], axis=-1, keepdims=True)), sink[g]) for g in groups]
    p_c = [jnp.exp(l_c[g] - m[g]) for g in groups]
    p_p = [jnp.exp(l_p[g] - m[g]) for g in groups]
    p_m = [jnp.exp(l_m[g] - m[g]) for g in groups]
    inv = [1.0 / (jnp.sum(p_c[g] + p_p[g], axis=-1, keepdims=True) + jnp.sum(p_m[g], axis=-1, keepdims=True)
                  + jnp.exp(sink[g] - m[g])) for g in groups]
    o4 = [(_dot((p_c[g] * inv[g]).astype(BF16), vc_ref[:, cols[g]])
           + _dot((p_p[g] * inv[g]).astype(BF16), vp_ref[:, cols[g]])
           + _dot((p_m[g] * inv[g]).astype(BF16), vm_ref[:, cols[g]])).astype(BF16) for g in groups]
    o_ref[...] = jnp.concatenate([o4[g][j * BLOCK:(j + 1) * BLOCK] for g in groups for j in range(ATT_GROUP)], axis=1)


def _attention(qkv, kmeta, vmeta, sinks):
    b, s, _ = qkv.shape
    nb = s // BLOCK
    kcol, vcol = ATT_Q // ATT_KV, ATT_Q // ATT_KV + 1
    prev = lambda n: jnp.maximum(n - 1, 0)
    return pl.pallas_call(
        _attn_body,
        grid=(b, nb),
        in_specs=[
            pl.BlockSpec(memory_space=pltpu.SMEM),
            pl.BlockSpec((None, BLOCK, ATT_Q), lambda i, n: (i, n, 0)),
            pl.BlockSpec((None, BLOCK, ATT_KV), lambda i, n: (i, n, kcol)),
            pl.BlockSpec((None, BLOCK, ATT_KV), lambda i, n: (i, prev(n), kcol)),
            pl.BlockSpec((None, BLOCK, ATT_KV), lambda i, n: (i, n, vcol)),
            pl.BlockSpec((None, BLOCK, ATT_KV), lambda i, n: (i, prev(n), vcol)),
            pl.BlockSpec((N_META, ATT_KV), lambda i, n: (0, 0)),
            pl.BlockSpec((N_META, ATT_KV), lambda i, n: (0, 0)),
        ],
        out_specs=pl.BlockSpec((None, BLOCK, ATT_Q), lambda i, n: (i, n, 0)),
        out_shape=jax.ShapeDtypeStruct((b, s, ATT_Q), BF16),
        compiler_params=_cparams("parallel", "parallel"),
        name="attn",
    )(sinks, qkv, qkv, qkv, qkv, qkv, kmeta, vmeta)


def _split(x):
    hi = x.astype(BF16)
    return hi, (x - hi.astype(F32)).astype(BF16)


def _seg_dot(parts, bd):
    hi, lo = parts
    cols = range(hi.shape[1] // LANES)
    return jnp.concatenate([_dot(hi[:, p * LANES:(p + 1) * LANES], bd) + _dot(lo[:, p * LANES:(p + 1) * LANES], bd)
                            for p in cols], axis=1)


def _rwkv_prep(rw_ref, prev_ref, shift_ref, rowp_ref, lora_ref, bd_ref, dst, wr, b):
    ar_s, bt_s, kt_s, bke_s, vb_s, bonus_s, g_s, plast_s = dst
    L = CHUNK
    w0, a0, kk, ka, rk = (rowp_ref[i:i + 1, :] for i in range(5))
    wdec = lora_ref[:DECAY_LORA, :]
    waaa = lora_ref[DECAY_LORA:DECAY_LORA + AAA_LORA, :]
    wgate = lora_ref[DECAY_LORA + AAA_LORA:, :]
    bd = bd_ref[...]
    c = rw_ref[b]
    row = lax.broadcasted_iota(jnp.int32, (L, 1), 0)
    cp = jnp.where(row == 0, prev_ref[b], pltpu.roll(c, 1, 0))
    prev_ref[b] = c[L - 1:L, :]
    x = c + shift_ref[0:1, :] * (cp - c)
    r = x[:, :RWKV_DIM]
    k = x[:, RWKV_DIM:2 * RWKV_DIM]
    v = x[:, 2 * RWKV_DIM:3 * RWKV_DIM]
    o = 3 * RWKV_DIM
    tw = jnp.tanh(x[:, o:o + DECAY_LORA]).astype(BF16)
    xa = x[:, o + DECAY_LORA:o + DECAY_LORA + AAA_LORA].astype(BF16)
    sg = _sigmoid(x[:, o + DECAY_LORA + AAA_LORA:]).astype(BF16)
    yield
    z = -(w0 + _dot(tw, wdec))
    a = _sigmoid(a0 + _dot(xa, waaa))
    g_s[wr, b] = _dot(sg, wgate)
    yield
    softplus = jnp.maximum(z, 0.0) + jnp.log(1.0 + jnp.exp(-jnp.abs(z)))
    lw = -jnp.exp(-softplus - 0.5)
    kkr = k * kk
    k2 = k * (1.0 + (a - 1.0) * ka)

    cum = lw
    shift = 1
    while shift < L:
        cum = cum + jnp.where(row >= shift, pltpu.roll(cum, shift, 0), 0.0)
        shift *= 2
    cum_last = cum[L - 1:L, :]
    p_inc = jnp.exp(cum)
    p_inv = jnp.exp(-cum)
    p_prev = jnp.exp(cum - lw)
    p_end = jnp.exp(cum_last - cum)
    plast_s[wr, b] = jnp.exp(cum_last)
    sq_parts = _split(kkr * kkr)
    bonus_parts = _split(r * k2 * rk)
    yield
    ssq = _seg_dot(sq_parts, bd)
    bsum = _seg_dot(bonus_parts, bd)
    yield
    kkn = kkr / jnp.maximum(jnp.sqrt(ssq), 1e-12)
    bv = kkn * a
    ar_s[wr, b, :L] = (-kkn * p_prev).astype(BF16)
    ar_s[wr, b, L:] = (r * p_inc).astype(BF16)
    bt_s[wr, b] = (bv * p_inv).astype(BF16)
    kt_s[wr, b] = (k2 * p_inv).astype(BF16)
    bke_s[wr, b, :L] = (bv * p_end).astype(BF16)
    bke_s[wr, b, L:] = (k2 * p_end).astype(BF16)
    vb_s[wr, b] = v.astype(BF16)
    bonus_s[wr, b] = bsum * v


def _rwkv_chain(src, rd, state_ref, rowp_ref, bd_ref, yg_ref):
    ar_s, bt_s, kt_s, bke_s, vb_s, bonus_s, g_s, plast_s = src
    L = CHUNK
    rows = yg_ref.shape[0]
    pairs = RWKV_DIM // LANES
    inst = [(b, p) for b in range(rows) for p in range(pairs)]
    cs = [slice(p * LANES, (p + 1) * LANES) for p in range(pairs)]
    n = range(len(inst))
    bd = bd_ref[...]
    lng, lnb = rowp_ref[5:6, :], rowp_ref[6:7, :]
    ti = lax.broadcasted_iota(jnp.int32, (2 * L, 2 * L), 0)
    tj = lax.broadcasted_iota(jnp.int32, (2 * L, 2 * L), 1)
    strict = ti > tj
    incl = ti >= tj
    eye = (ti == tj).astype(F32)
    same_head = (ti < L) == (tj < L)
    head_a = lax.broadcasted_iota(jnp.int32, (L, LANES), 1) < RWKV_HEAD_DIM

    def expand(x):
        zero = jnp.zeros_like(x)
        return jnp.concatenate([jnp.where(head_a, x, zero), jnp.where(head_a, zero, x)], axis=0)

    def fold(x):
        return x[:L] + x[L:]

    ar = [ar_s[rd, b, :, cs[p]] for b, p in inst]
    xa = [expand(t[:L]) for t in ar]
    xr = [expand(t[L:]) for t in ar]
    yb = [expand(bt_s[rd, b, :, cs[p]]) for b, p in inst]
    yk = [expand(kt_s[rd, b, :, cs[p]]) for b, p in inst]
    vb = [vb_s[rd, b, :, cs[p]] for b, p in inst]
    vm = [expand(t) for t in vb]
    s_old = [state_ref[i] for i in n]
    sob = [t.astype(BF16) for t in s_old]

    a_ab = [jnp.where(strict, _dot_nt(xa[i], yb[i]), 0.0) for i in n]
    a_ak = [jnp.where(strict, _dot_nt(xa[i], yk[i]), 0.0).astype(BF16) for i in n]
    a_rb = [jnp.where(incl, _dot_nt(xr[i], yb[i]), 0.0).astype(BF16) for i in n]
    a_rk = [jnp.where(incl, _dot_nt(xr[i], yk[i]), 0.0).astype(BF16) for i in n]
    ars = [_dot_nt(ar[i], sob[i]) for i in n]
    akv = [fold(_dot(a_ak[i], vm[i])) for i in n]
    rkv = [fold(_dot(a_rk[i], vm[i])) for i in n]
    yield

    t_inv = [eye + a_ab[i] for i in n]
    ab = [t.astype(BF16) for t in a_ab]
    pk = [_dot(ab[i], ab[i]) for i in n]
    for step in range(5):
        pkb = [t.astype(BF16) for t in pk]
        t_inv = [t_inv[i] + _dot(t_inv[i].astype(BF16), pkb[i]) for i in n]
        if step < 4:
            pk = [_dot(pkb[i], pkb[i]) for i in n]
        yield

    zm = [expand((ars[i][:L] + akv[i]).astype(BF16)) for i in n]
    u_st = [_dot(t_inv[i].astype(BF16), zm[i]).astype(BF16) for i in n]
    y = [ars[i][L:] + rkv[i] + fold(_dot(a_rb[i], u_st[i])) for i in n]
    uv = [jnp.concatenate([fold(u_st[i]), vb[i]], axis=0) for i in n]
    upd = [_dot_tn(uv[i], bke_s[rd, b, :, cs[p]]) for i, (b, p) in enumerate(inst)]
    for i, (b, p) in enumerate(inst):
        state_ref[i] = jnp.where(same_head, s_old[i] * plast_s[rd, b][:, cs[p]] + upd[i], 0.0)

    inv_n = 1.0 / RWKV_HEAD_DIM
    for i, (b, p) in enumerate(inst):
        yc = y[i] - _seg_dot(_split(y[i]), bd) * inv_n
        var = _seg_dot(_split(yc * yc), bd) * inv_n
        yn = yc * lax.rsqrt(var + LN_X_EPS) * lng[:, cs[p]] + lnb[:, cs[p]]
        yg_ref[b, :, cs[p]] = ((yn + bonus_s[rd, b, :, cs[p]]) * g_s[rd, b, :, cs[p]]).astype(BF16)


def _rwkv_body(rw_ref, shift_ref, s0_ref, rowp_ref, lora_ref, bd_ref, yg_ref, sfin_ref, state_ref, prev_ref, *slots):
    ci = pl.program_id(1)
    rows = rw_ref.shape[0]
    pairs = RWKV_DIM // LANES

    @pl.when(ci == 0)
    def _():
        for b in range(rows):
            state_ref[b * pairs:(b + 1) * pairs] = s0_ref[...]
            prev_ref[b] = shift_ref[1:2, :]
        for ref in slots[:-1]:
            ref[0] = jnp.zeros(ref.shape[1:], ref.dtype)
        slots[-1][0] = jnp.ones(slots[-1].shape[1:], F32)

    def step(rd, wr):
        chain = _rwkv_chain(slots, rd, state_ref, rowp_ref, bd_ref, yg_ref)
        preps = [_rwkv_prep(rw_ref, prev_ref, shift_ref, rowp_ref, lora_ref, bd_ref, slots, wr, b)
                 for b in range(rows)]
        for g in preps:
            next(g)
        next(chain)
        for g in preps:
            next(g)
        for g in preps:
            next(g)
        next(chain)
        next(chain)
        for g in preps:
            next(g)
        for g in preps:
            next(g, None)
        next(chain)
        next(chain)
        next(chain)
        next(chain, None)

    @pl.when(ci % 2 == 0)
    def _():
        step(0, 1)

    @pl.when(ci % 2 == 1)
    def _():
        step(1, 0)

    @pl.when(ci == pl.num_programs(1) - 1)
    def _():
        sfin_ref[...] = state_ref[...]


def _rwkv(rw, cprev0, s0, p):
    b, s, _ = rw.shape
    nc = s // CHUNK
    L = CHUNK
    pairs = RWKV_DIM // LANES
    rows = RWKV_ROWS if b % RWKV_ROWS == 0 else 1
    shift = jnp.concatenate([p["mu"], cprev0], axis=0)
    lane_head = jnp.arange(LANES) // RWKV_HEAD_DIM
    bd = (lane_head[:, None] == lane_head[None, :]).astype(BF16)
    full = lambda shape: pl.BlockSpec(shape, lambda i, c: (0,) * len(shape))
    return pl.pallas_call(
        _rwkv_body,
        grid=(b // rows, nc + 1),
        in_specs=[
            pl.BlockSpec((rows, CHUNK, RWKV_COLS), lambda i, c: (i, jnp.minimum(c, nc - 1), 0)),
            full(shift.shape),
            full((pairs, LANES, LANES)),
            full(p["rows"].shape),
            full(p["lora"].shape),
            full((LANES, LANES)),
        ],
        out_specs=[
            pl.BlockSpec((rows, CHUNK, RWKV_DIM), lambda i, c: (i, jnp.maximum(c - 1, 0), 0)),
            pl.BlockSpec((rows * pairs, LANES, LANES), lambda i, c: (i, 0, 0)),
        ],
        out_shape=[
            jax.ShapeDtypeStruct((b, s, RWKV_DIM), BF16),
            jax.ShapeDtypeStruct((b * pairs, LANES, LANES), F32),
        ],
        scratch_shapes=[
            pltpu.VMEM((rows * pairs, LANES, LANES), F32),
            pltpu.VMEM((rows, 1, RWKV_COLS), F32),
            pltpu.VMEM((2, rows, 2 * L, RWKV_DIM), BF16),
            pltpu.VMEM((2, rows, L, RWKV_DIM), BF16),
            pltpu.VMEM((2, rows, L, RWKV_DIM), BF16),
            pltpu.VMEM((2, rows, 2 * L, RWKV_DIM), BF16),
            pltpu.VMEM((2, rows, L, RWKV_DIM), BF16),
            pltpu.VMEM((2, rows, L, RWKV_DIM), F32),
            pltpu.VMEM((2, rows, L, RWKV_DIM), F32),
            pltpu.VMEM((2, rows, 1, RWKV_DIM), F32),
        ],
        compiler_params=_cparams("arbitrary", "arbitrary"),
        name="rwkv",
    )(rw, shift, s0, p["rows"], p["lora"], bd)


def _merge_body(x_ref, oa_ref, yg_ref, gate_ref, woa_ref, wor_ref, wout_ref, g2_ref, wrh_ref, wrl_ref, br_ref,
                h1_ref, xn_ref, info_ref, cnt_ref, base_ref):
    tm = x_ref.shape[0]

    @pl.when(pl.program_id(0) == 0)
    def _():
        base_ref[...] = jnp.zeros_like(base_ref)

    att = _dot(oa_ref[...], woa_ref[...])
    rwk = _dot(yg_ref[...], wor_ref[...])
    gate = gate_ref[...].astype(F32)
    merged = _sigmoid(gate[:, :D_MODEL]) * att + _sigmoid(gate[:, D_MODEL:]) * rwk
    h1 = x_ref[...] + _dot(merged.astype(BF16), wout_ref[...])
    h1_ref[...] = h1
    xn = h1 * lax.rsqrt(jnp.mean(h1 * h1, axis=-1, keepdims=True) + NORM_EPS) * g2_ref[...]
    xn_ref[...] = xn

    xn_hi = xn.astype(BF16)
    xn_lo = (xn - xn_hi.astype(F32)).astype(BF16)
    logits = (_dot(xn_hi, wrh_ref[...]) + _dot(xn_lo, wrh_ref[...]) + _dot(xn_hi, wrl_ref[...])) + br_ref[...]
    lane = lax.broadcasted_iota(jnp.int32, (tm, LANES), 1)
    neg_inf = jnp.float32(-jnp.inf)
    big = jnp.int32(LANES)
    gl = jnp.where(lane < N_GROUPS, logits, neg_inf)
    gmax = jnp.max(gl, axis=-1, keepdims=True)
    gsel = jnp.min(jnp.where(gl == gmax, lane, big), axis=-1, keepdims=True)
    p_sel = 1.0 / jnp.sum(jnp.exp(gl - gmax), axis=-1, keepdims=True)
    lo = ROUTER_LANE0 + gsel * EXPERTS_PER_GROUP
    el = jnp.where(jnp.logical_and(lane >= lo, lane < lo + EXPERTS_PER_GROUP), logits, neg_inf)
    v1 = jnp.max(el, axis=-1, keepdims=True)
    i1 = jnp.min(jnp.where(el == v1, lane, big), axis=-1, keepdims=True)
    el2 = jnp.where(lane == i1, neg_inf, el)
    v2 = jnp.max(el2, axis=-1, keepdims=True)
    i2 = jnp.min(jnp.where(el2 == v2, lane, big), axis=-1, keepdims=True)
    e2 = jnp.exp(v2 - v1)
    w1 = p_sel / (1.0 + e2)
    w2 = p_sel * e2 / (1.0 + e2)

    oh1 = (lane == i1).astype(F32)
    oh2 = (lane == i2).astype(F32)
    both = oh1 + oh2
    ti = lax.broadcasted_iota(jnp.int32, (tm, tm), 0)
    tj = lax.broadcasted_iota(jnp.int32, (tm, tm), 1)
    before = _dot((ti > tj).astype(BF16), both.astype(BF16)) + base_ref[...]
    rank1 = jnp.sum(before * oh1, axis=-1, keepdims=True)
    rank2 = jnp.sum(before * oh2, axis=-1, keepdims=True)
    base = base_ref[...] + jnp.sum(both, axis=0, keepdims=True)
    base_ref[...] = base
    cnt_ref[...] = base

    info = jnp.where(lane == 0, (i1 - ROUTER_LANE0).astype(F32), 0.0)
    info = jnp.where(lane == 1, (i2 - ROUTER_LANE0).astype(F32), info)
    info = jnp.where(lane == 2, w1, info)
    info = jnp.where(lane == 3, w2, info)
    info = jnp.where(lane == 4, rank1, info)
    info = jnp.where(lane == 5, rank2, info)
    info_ref[...] = info[:, :INFO_COLS]


def _merge(x2d, oa, yg, gate, woa, wor, wout, g2, wr, br, tm):
    wr_hi = wr.astype(BF16)
    wr_lo = (wr - wr_hi.astype(F32)).astype(BF16)
    m = x2d.shape[0]
    tile = lambda n: pl.BlockSpec((tm, n), lambda i: (i, 0))
    full = lambda a, b: pl.BlockSpec((a, b), lambda i: (0, 0))
    return pl.pallas_call(
        _merge_body,
        grid=(m // tm,),
        in_specs=[
            tile(D_MODEL), tile(ATT_Q), tile(RWKV_DIM), tile(GATE_COLS),
            full(ATT_Q, D_MODEL), full(RWKV_DIM, D_MODEL), full(D_MODEL, D_MODEL),
            full(1, D_MODEL), full(D_MODEL, LANES), full(D_MODEL, LANES), full(1, LANES),
        ],
        out_specs=[tile(D_MODEL), tile(D_MODEL), tile(INFO_COLS), full(1, LANES)],
        out_shape=[
            jax.ShapeDtypeStruct((m, D_MODEL), F32),
            jax.ShapeDtypeStruct((m, D_MODEL), F32),
            jax.ShapeDtypeStruct((m, INFO_COLS), F32),
            jax.ShapeDtypeStruct((1, LANES), F32),
        ],
        scratch_shapes=[pltpu.VMEM((1, LANES), F32)],
        compiler_params=_cparams("arbitrary"),
        name="merge",
    )(x2d, oa, yg, gate, woa, wor, wout, g2, wr_hi, wr_lo, br)


def _dispatch_body(max_trailing, pends_ref, dest_ref, xn_ref, xs_ref, zero_ref, sem, zsem):
    tm = xn_ref.shape[0]
    cap_blocks = xs_ref.shape[0] // SLOT_BLOCK

    @pl.when(pl.program_id(0) == 0)
    def _():
        zero_ref[...] = jnp.zeros_like(zero_ref)

        def zero_block(start):
            dst = xs_ref.at[pl.ds(pl.multiple_of(start, SLOT_BLOCK), SLOT_BLOCK)]
            return pltpu.make_async_copy(zero_ref, dst, zsem)

        ends = [pends_ref[e] for e in range(N_EXPERTS)]
        non_empty = [ends[e] > (ends[e - 1] if e else 0) for e in range(N_EXPERTS)]
        used_blocks = ends[-1] // SLOT_BLOCK
        trailing = [used_blocks + j < cap_blocks for j in range(max_trailing)]
        for e in range(N_EXPERTS):
            @pl.when(non_empty[e])
            def _():
                zero_block(ends[e] - SLOT_BLOCK).start()
        for j in range(max_trailing):
            @pl.when(trailing[j])
            def _():
                zero_block((used_blocks + j) * SLOT_BLOCK).start()
        for cond in non_empty + trailing:
            @pl.when(cond)
            def _():
                zero_block(0).wait()

    def row_copy(r, d):
        return pltpu.make_async_copy(xn_ref.at[pl.ds(r, 1)], xs_ref.at[pl.ds(d, 1)], sem)

    def start(r, carry):
        row_copy(r, dest_ref[0, 0, 2 * r]).start()
        row_copy(r, dest_ref[0, 0, 2 * r + 1]).start()
        return carry

    def wait(r, carry):
        row_copy(0, 0).wait()
        row_copy(0, 0).wait()
        return carry

    lax.fori_loop(0, tm, start, 0, unroll=DMA_UNROLL)
    lax.fori_loop(0, tm, wait, 0, unroll=DMA_UNROLL)


def _dispatch(pends, dest3, xn, cap):
    tm = dest3.shape[2] // TOP_K
    max_trailing = cap // SLOT_BLOCK - (dest3.shape[0] * dest3.shape[2]) // SLOT_BLOCK
    return pl.pallas_call(
        functools.partial(_dispatch_body, max_trailing),
        grid_spec=pltpu.PrefetchScalarGridSpec(
            num_scalar_prefetch=1,
            grid=(dest3.shape[0],),
            in_specs=[
                pl.BlockSpec((1, 1, dest3.shape[2]), lambda i, pe: (i, 0, 0), memory_space=pltpu.SMEM),
                pl.BlockSpec((tm, D_MODEL), lambda i, pe: (i, 0)),
            ],
            out_specs=pl.BlockSpec(memory_space=pl.ANY),
            scratch_shapes=[pltpu.VMEM((SLOT_BLOCK, D_MODEL), F32), pltpu.SemaphoreType.DMA(()),
                            pltpu.SemaphoreType.DMA(())],
        ),
        out_shape=jax.ShapeDtypeStruct((cap, D_MODEL), F32),
        compiler_params=_cparams("arbitrary"),
        name="dispatch",
    )(pends, dest3, xn)


def _expert_body(be_ref, xs_ref, wg_ref, wu_ref, wd_ref, yb_ref):
    del be_ref
    xb = xs_ref[...].astype(BF16)
    gt = _dot(xb, wg_ref[...])
    up = _dot(xb, wu_ref[...])
    hmid = gt * _sigmoid(gt) * up
    yb_ref[...] = _dot(hmid.astype(BF16), wd_ref[...])


def _experts(block_expert, xs, wg, wu, wd):
    cap = xs.shape[0]
    nblk = cap // SLOT_BLOCK
    return pl.pallas_call(
        _expert_body,
        grid_spec=pltpu.PrefetchScalarGridSpec(
            num_scalar_prefetch=1,
            grid=(nblk,),
            in_specs=[
                pl.BlockSpec((SLOT_BLOCK, D_MODEL), lambda i, be: (i, 0)),
                pl.BlockSpec((None, D_MODEL, D_EXPERT), lambda i, be: (be[i], 0, 0)),
                pl.BlockSpec((None, D_MODEL, D_EXPERT), lambda i, be: (be[i], 0, 0)),
                pl.BlockSpec((None, D_EXPERT, D_MODEL), lambda i, be: (be[i], 0, 0)),
            ],
            out_specs=pl.BlockSpec((SLOT_BLOCK, D_MODEL), lambda i, be: (i, 0)),
        ),
        out_shape=jax.ShapeDtypeStruct((cap, D_MODEL), F32),
        compiler_params=_cparams("arbitrary"),
        name="experts",
    )(block_expert, xs, wg, wu, wd)


def _combine_body(dcur_ref, dnext_ref, h1_ref, info_ref, gf_ref, yb_ref, out_ref, buf_ref, sem):
    i = pl.program_id(0)
    tm = h1_ref.shape[0]

    def row_copy(d, slot, k, r):
        return pltpu.make_async_copy(yb_ref.at[pl.ds(d, 1)], buf_ref.at[slot, k, pl.ds(r, 1)], sem.at[slot])

    def issue(dref, slot):
        def start(r, carry):
            row_copy(dref[0, 0, 2 * r], slot, 0, r).start()
            row_copy(dref[0, 0, 2 * r + 1], slot, 1, r).start()
            return carry
        lax.fori_loop(0, tm, start, 0, unroll=DMA_UNROLL)

    def step(slot):
        @pl.when(i == 0)
        def _():
            issue(dcur_ref, slot)

        @pl.when(i + 1 < pl.num_programs(0))
        def _():
            issue(dnext_ref, 1 - slot)

        def wait(r, carry):
            row_copy(0, slot, 0, 0).wait()
            row_copy(0, slot, 1, 0).wait()
            return carry
        lax.fori_loop(0, tm, wait, 0, unroll=DMA_UNROLL)
        info = info_ref[...]
        y = buf_ref[slot, 0] * info[:, 2:3] + buf_ref[slot, 1] * info[:, 3:4]
        h2 = h1_ref[...] + y
        out_ref[...] = h2 * lax.rsqrt(jnp.mean(h2 * h2, axis=-1, keepdims=True) + NORM_EPS) * gf_ref[...]

    @pl.when(i % 2 == 0)
    def _():
        step(0)

    @pl.when(i % 2 == 1)
    def _():
        step(1)


def _combine(dest3, h1, info, gf, yb):
    m = h1.shape[0]
    nt = dest3.shape[0]
    tm = m // nt
    dspec = lambda f: pl.BlockSpec((1, 1, dest3.shape[2]), f, memory_space=pltpu.SMEM)
    return pl.pallas_call(
        _combine_body,
        grid=(nt,),
        in_specs=[
            dspec(lambda i: (i, 0, 0)),
            dspec(lambda i: (jnp.minimum(i + 1, nt - 1), 0, 0)),
            pl.BlockSpec((tm, D_MODEL), lambda i: (i, 0)),
            pl.BlockSpec((tm, INFO_COLS), lambda i: (i, 0)),
            pl.BlockSpec((1, D_MODEL), lambda i: (0, 0)),
            pl.BlockSpec(memory_space=pl.ANY),
        ],
        out_specs=pl.BlockSpec((tm, D_MODEL), lambda i: (i, 0)),
        out_shape=jax.ShapeDtypeStruct((m, D_MODEL), F32),
        scratch_shapes=[pltpu.VMEM((2, TOP_K, tm, D_MODEL), F32), pltpu.SemaphoreType.DMA((2,))],
        compiler_params=_cparams("arbitrary"),
        name="combine",
    )(dest3, dest3, h1, info, gf, yb)


def _tile(m, pref):
    t = pref
    while m % t:
        t //= 2
    return t


def kernel(x, meta_tokens, norm_mix_g, w_in, shift_mu, w0, w_decay_up, a0, w_aaa_up, w_gate_up, k_k, k_a, r_k,
           ln_x_g, ln_x_b, attn_sinks, w_o_attn, w_o_rwkv, w_out, norm_ffn_g, w_grp, b_grp, w_exp, b_exp,
           e_gate, e_up, e_down, norm_final_g):
    assert norm_mix_g.shape[0] == 1, "single-layer trunk"
    b, s, d = x.shape
    assert d == D_MODEL and s % BLOCK == 0
    m = b * s
    row = lambda t: t.reshape(1, -1).astype(F32)
    w_in_b = w_in[0].astype(BF16)
    g_mix = row(norm_mix_g[0])
    rp = dict(mu=row(shift_mu[0]),
              rows=jnp.concatenate([row(t[0]) for t in (w0, a0, k_k, k_a, r_k, ln_x_g, ln_x_b)]
                                   + [jnp.zeros((1, RWKV_DIM), F32)], axis=0),
              lora=jnp.concatenate([w_decay_up[0], w_aaa_up[0], w_gate_up[0]], axis=0).astype(BF16))

    x0 = jnp.concatenate([jnp.zeros((BLOCK - N_META, d), F32), meta_tokens.astype(F32)], axis=0)
    qkv0, rw0, _ = _proj(x0, g_mix, w_in_b, BLOCK)
    kmeta = qkv0[BLOCK - N_META:, ATT_Q:ATT_Q + ATT_KV]
    vmeta = qkv0[BLOCK - N_META:, ATT_Q + ATT_KV:]
    zero_state = jnp.zeros((RWKV_DIM // LANES, LANES, LANES), F32)
    _, s_meta = _rwkv(rw0[None], jnp.zeros((1, RWKV_COLS), F32), zero_state, rp)

    x2d = x.reshape(m, d)
    qkv, rw, gate = _proj(x2d, g_mix, w_in_b, _tile(m, ROW_TILE))
    o_att = _attention(qkv.reshape(b, s, ATT_COLS), kmeta, vmeta, attn_sinks[0].astype(F32))
    yg, _ = _rwkv(rw.reshape(b, s, RWKV_COLS), rw0[BLOCK - 1:], s_meta, rp)

    wr = jnp.zeros((d, LANES), F32).at[:, :N_GROUPS].set(w_grp[0]).at[:, N_GROUPS:N_GROUPS + N_EXPERTS].set(w_exp[0])
    br = jnp.zeros((1, LANES), F32).at[0, :N_GROUPS].set(b_grp[0]).at[0, N_GROUPS:N_GROUPS + N_EXPERTS].set(b_exp[0])
    tm = _tile(m, DMA_TILE)
    h1, xn2, info, cnt = _merge(x2d, o_att.reshape(m, ATT_Q), yg.reshape(m, RWKV_DIM), gate,
                                w_o_attn[0].astype(BF16), w_o_rwkv[0].astype(BF16), w_out[0].astype(BF16),
                                row(norm_ffn_g[0]), wr, br, _tile(m, ROW_TILE))

    counts = cnt[0, N_GROUPS:N_GROUPS + N_EXPERTS].astype(jnp.int32)
    padded = ((counts + SLOT_BLOCK - 1) // SLOT_BLOCK) * SLOT_BLOCK
    pends = jnp.cumsum(padded)
    pstarts = pends - padded
    eid = info[:, 0:2].astype(jnp.int32)
    dest = pstarts[eid] + info[:, 4:6].astype(jnp.int32)
    nblk = -(-(m * TOP_K + N_EXPERTS * (SLOT_BLOCK - 1)) // SLOT_BLOCK)
    cap = nblk * SLOT_BLOCK
    block_start = jnp.arange(nblk, dtype=jnp.int32) * SLOT_BLOCK
    block_expert = jnp.minimum(jnp.sum(block_start[:, None] >= pends[None, :], axis=1), N_EXPERTS - 1).astype(jnp.int32)
    dest3 = dest.reshape(m // tm, 1, 2 * tm)

    xs = _dispatch(pends.astype(jnp.int32), dest3, xn2, cap)
    yb = _experts(block_expert, xs, e_gate[0].astype(BF16), e_up[0].astype(BF16), e_down[0].astype(BF16))
    out = _combine(dest3, h1, info, row(norm_final_g), yb)
    return out.reshape(b, s, d)
```

```python
import functools

import numpy as np
import jax
import jax.numpy as jnp
from jax import lax
from jax.experimental import pallas as pl
from jax.experimental.pallas import tpu as pltpu

F32 = jnp.float32
BF16 = jnp.bfloat16

D_MODEL = 1024
N_META = 16
ATT_HEADS = 8
ATT_KV_HEADS = 2
ATT_GROUP = ATT_HEADS // ATT_KV_HEADS
HEAD_DIM = 64
WINDOW = 128
BLOCK = 128
RWKV_HEADS = 8
RWKV_HEAD_DIM = 64
RWKV_DIM = RWKV_HEADS * RWKV_HEAD_DIM
DECAY_LORA = 64
AAA_LORA = 64
GATE_LORA = 128
LN_X_EPS = 64e-5
ATT_Q = ATT_HEADS * HEAD_DIM
ATT_KV = ATT_KV_HEADS * HEAD_DIM
ATT_COLS = ATT_Q + 2 * ATT_KV
RWKV_COLS = 3 * RWKV_DIM + DECAY_LORA + AAA_LORA + GATE_LORA
GATE_COLS = 2 * D_MODEL
IN_COLS = ATT_COLS + RWKV_COLS + GATE_COLS
N_GROUPS = 4
EXPERTS_PER_GROUP = 8
N_EXPERTS = N_GROUPS * EXPERTS_PER_GROUP
TOP_K = 2
D_EXPERT = 256
NORM_EPS = 1e-6

LANES = 128
SUBLANES = 8
CHUNK = 64
RWKV_ROWS = 4
SLOT_BLOCK = 512
ROW_TILE = 512
DMA_TILE = 256
DMA_UNROLL = 8
ROUTER_LANE0 = N_GROUPS
INFO_COLS = 8
VMEM_LIMIT = 56 * 1024 * 1024


def _cparams(*sem):
    return pltpu.CompilerParams(dimension_semantics=sem, vmem_limit_bytes=VMEM_LIMIT)


def _dot(a, b):
    return jnp.dot(a, b, preferred_element_type=F32)


def _dot_nt(a, b):
    return lax.dot_general(a, b, (((1,), (1,)), ((), ())), preferred_element_type=F32)


def _dot_tn(a, b):
    return lax.dot_general(a, b, (((0,), (0,)), ((), ())), preferred_element_type=F32)


def _sigmoid(x):
    return 0.5 * jnp.tanh(0.5 * x) + 0.5


def _proj_body(x_ref, g_ref, w_ref, qkv_ref, rw_ref, gate_ref):
    x = x_ref[...]
    y = x * lax.rsqrt(jnp.mean(x * x, axis=-1, keepdims=True) + NORM_EPS)
    xb = (y * g_ref[...]).astype(BF16)
    qkv_ref[...] = _dot(xb, w_ref[:, :ATT_COLS]).astype(BF16)
    rw_ref[...] = _dot(xb, w_ref[:, ATT_COLS:ATT_COLS + RWKV_COLS])
    gate_ref[...] = _dot(xb, w_ref[:, ATT_COLS + RWKV_COLS:]).astype(BF16)


def _proj(x2d, g, w_bf16, tm):
    m = x2d.shape[0]
    return pl.pallas_call(
        _proj_body,
        grid=(m // tm,),
        in_specs=[
            pl.BlockSpec((tm, D_MODEL), lambda i: (i, 0)),
            pl.BlockSpec((1, D_MODEL), lambda i: (0, 0)),
            pl.BlockSpec((D_MODEL, IN_COLS), lambda i: (0, 0)),
        ],
        out_specs=[
            pl.BlockSpec((tm, ATT_COLS), lambda i: (i, 0)),
            pl.BlockSpec((tm, RWKV_COLS), lambda i: (i, 0)),
            pl.BlockSpec((tm, GATE_COLS), lambda i: (i, 0)),
        ],
        out_shape=[
            jax.ShapeDtypeStruct((m, ATT_COLS), BF16),
            jax.ShapeDtypeStruct((m, RWKV_COLS), F32),
            jax.ShapeDtypeStruct((m, GATE_COLS), BF16),
        ],
        compiler_params=_cparams("parallel"),
        name="proj",
    )(x2d, g, w_bf16)


def _attn_body(sink_ref, q_ref, kc_ref, kp_ref, vc_ref, vp_ref, km_ref, vm_ref, o_ref):
    n = pl.program_id(1)
    rows = ATT_GROUP * BLOCK
    ri = lax.broadcasted_iota(jnp.int32, (rows, BLOCK), 0)
    kj = lax.broadcasted_iota(jnp.int32, (rows, BLOCK), 1)
    d_cur = (ri % BLOCK) - kj
    valid_cur = d_cur >= 0
    valid_prev = jnp.logical_and(d_cur < 0, n > 0)
    f_cur = d_cur.astype(F32)
    f_prev = f_cur + float(BLOCK)
    rm = lax.broadcasted_iota(jnp.int32, (rows, N_META), 0)
    mi = lax.broadcasted_iota(jnp.int32, (rows, N_META), 1)
    f_meta = jnp.minimum(N_META + n * BLOCK + (rm % BLOCK) - mi, WINDOW).astype(F32)
    head_in_group = lax.broadcasted_iota(jnp.int32, (rows, 1), 0) // BLOCK
    scale = HEAD_DIM ** -0.5
    neg_inf = jnp.float32(-jnp.inf)
    groups = range(ATT_KV_HEADS)
    cols = [slice(g * HEAD_DIM, (g + 1) * HEAD_DIM) for g in groups]

    slope, sink = [], []
    for g in groups:
        sl_g = jnp.zeros((rows, 1), F32)
        sk_g = jnp.zeros((rows, 1), F32)
        for j in range(ATT_GROUP):
            h = g * ATT_GROUP + j
            sl_g = jnp.where(head_in_group == j, float(np.float32(2.0 ** (-8.0 * (h + 1) / ATT_HEADS))), sl_g)
            sk_g = jnp.where(head_in_group == j, sink_ref[h], sk_g)
        slope.append(sl_g)
        sink.append(sk_g)

    q4 = [jnp.concatenate([q_ref[:, (g * ATT_GROUP + j) * HEAD_DIM:(g * ATT_GROUP + j + 1) * HEAD_DIM]
                           for j in range(ATT_GROUP)], axis=0) for g in groups]
    q4 = [t * jnp.asarray(scale, BF16) for t in q4]
    s_c = [_dot_nt(q4[g], kc_ref[:, cols[g]]) for g in groups]
    s_p = [_dot_nt(q4[g], kp_ref[:, cols[g]]) for g in groups]
    s_m = [_dot_nt(q4[g], km_ref[:, cols[g]]) for g in groups]
    l_c = [jnp.where(valid_cur, s_c[g] - slope[g] * f_cur, neg_inf) for g in groups]
    l_p = [jnp.where(valid_prev, s_p[g] - slope[g] * f_prev, neg_inf) for g in groups]
    l_m = [s_m[g] - slope[g] * f_meta for g in groups]
    m = [jnp.maximum(jnp.maximum(jnp.max(jnp.maximum(l_c[g], l_p[g]), axis=-1, keepdims=True),
                                 jnp.max(l_m[g], axis=-1, keepdims=True)), sink[g]) for g in groups]
    p_c = [jnp.exp(l_c[g] - m[g]) for g in groups]
    p_p = [jnp.exp(l_p[g] - m[g]) for g in groups]
    p_m = [jnp.exp(l_m[g] - m[g]) for g in groups]
    inv = [1.0 / (jnp.sum(p_c[g] + p_p[g], axis=-1, keepdims=True) + jnp.sum(p_m[g], axis=-1, keepdims=True)
                  + jnp.exp(sink[g] - m[g])) for g in groups]
    o4 = [(_dot((p_c[g] * inv[g]).astype(BF16), vc_ref[:, cols[g]])
           + _dot((p_p[g] * inv[g]).astype(BF16), vp_ref[:, cols[g]])
           + _dot((p_m[g] * inv[g]).astype(BF16), vm_ref[:, cols[g]])).astype(BF16) for g in groups]
    o_ref[...] = jnp.concatenate([o4[g][j * BLOCK:(j + 1) * BLOCK] for g in groups for j in range(ATT_GROUP)], axis=1)


def _attention(qkv, kmeta, vmeta, sinks):
    b, s, _ = qkv.shape
    nb = s // BLOCK
    kcol, vcol = ATT_Q // ATT_KV, ATT_Q // ATT_KV + 1
    prev = lambda n: jnp.maximum(n - 1, 0)
    return pl.pallas_call(
        _attn_body,
        grid=(b, nb),
        in_specs=[
            pl.BlockSpec(memory_space=pltpu.SMEM),
            pl.BlockSpec((None, BLOCK, ATT_Q), lambda i, n: (i, n, 0)),
            pl.BlockSpec((None, BLOCK, ATT_KV), lambda i, n: (i, n, kcol)),
            pl.BlockSpec((None, BLOCK, ATT_KV), lambda i, n: (i, prev(n), kcol)),
            pl.BlockSpec((None, BLOCK, ATT_KV), lambda i, n: (i, n, vcol)),
            pl.BlockSpec((None, BLOCK, ATT_KV), lambda i, n: (i, prev(n), vcol)),
            pl.BlockSpec((N_META, ATT_KV), lambda i, n: (0, 0)),
            pl.BlockSpec((N_META, ATT_KV), lambda i, n: (0, 0)),
        ],
        out_specs=pl.BlockSpec((None, BLOCK, ATT_Q), lambda i, n: (i, n, 0)),
        out_shape=jax.ShapeDtypeStruct((b, s, ATT_Q), BF16),
        compiler_params=_cparams("parallel", "parallel"),
        name="attn",
    )(sinks, qkv, qkv, qkv, qkv, qkv, kmeta, vmeta)


def _split(x):
    hi = x.astype(BF16)
    return hi, (x - hi.astype(F32)).astype(BF16)


def _seg_dot(parts, bd):
    hi, lo = parts
    cols = range(hi.shape[1] // LANES)
    return jnp.concatenate([_dot(hi[:, p * LANES:(p + 1) * LANES], bd) + _dot(lo[:, p * LANES:(p + 1) * LANES], bd)
                            for p in cols], axis=1)


def _rwkv_prep(rw_ref, prev_ref, shift_ref, rowp_ref, lora_ref, bd_ref, dst, wr, b):
    ar_s, bt_s, kt_s, bke_s, vb_s, bonus_s, g_s, plast_s = dst
    L = CHUNK
    w0, a0, kk, ka, rk = (rowp_ref[i:i + 1, :] for i in range(5))
    wdec = lora_ref[:DECAY_LORA, :]
    waaa = lora_ref[DECAY_LORA:DECAY_LORA + AAA_LORA, :]
    wgate = lora_ref[DECAY_LORA + AAA_LORA:, :]
    bd = bd_ref[...]
    c = rw_ref[b]
    row = lax.broadcasted_iota(jnp.int32, (L, 1), 0)
    cp = jnp.where(row == 0, prev_ref[b], pltpu.roll(c, 1, 0))
    prev_ref[b] = c[L - 1:L, :]
    x = c + shift_ref[0:1, :] * (cp - c)
    r = x[:, :RWKV_DIM]
    k = x[:, RWKV_DIM:2 * RWKV_DIM]
    v = x[:, 2 * RWKV_DIM:3 * RWKV_DIM]
    o = 3 * RWKV_DIM
    tw = jnp.tanh(x[:, o:o + DECAY_LORA]).astype(BF16)
    xa = x[:, o + DECAY_LORA:o + DECAY_LORA + AAA_LORA].astype(BF16)
    sg = _sigmoid(x[:, o + DECAY_LORA + AAA_LORA:]).astype(BF16)
    yield
    z = -(w0 + _dot(tw, wdec))
    a = _sigmoid(a0 + _dot(xa, waaa))
    g_s[wr, b] = _dot(sg, wgate)
    yield
    softplus = jnp.maximum(z, 0.0) + jnp.log(1.0 + jnp.exp(-jnp.abs(z)))
    lw = -jnp.exp(-softplus - 0.5)
    kkr = k * kk
    k2 = k * (1.0 + (a - 1.0) * ka)

    cum = lw
    shift = 1
    while shift < L:
        cum = cum + jnp.where(row >= shift, pltpu.roll(cum, shift, 0), 0.0)
        shift *= 2
    cum_last = cum[L - 1:L, :]
    p_inc = jnp.exp(cum)
    p_inv = jnp.exp(-cum)
    p_prev = jnp.exp(cum - lw)
    p_end = jnp.exp(cum_last - cum)
    plast_s[wr, b] = jnp.exp(cum_last)
    sq_parts = _split(kkr * kkr)
    bonus_parts = _split(r * k2 * rk)
    yield
    ssq = _seg_dot(sq_parts, bd)
    bsum = _seg_dot(bonus_parts, bd)
    yield
    kkn = kkr / jnp.maximum(jnp.sqrt(ssq), 1e-12)
    bv = kkn * a
    ar_s[wr, b, :L] = (-kkn * p_prev).astype(BF16)
    ar_s[wr, b, L:] = (r * p_inc).astype(BF16)
    bt_s[wr, b] = (bv * p_inv).astype(BF16)
    kt_s[wr, b] = (k2 * p_inv).astype(BF16)
    bke_s[wr, b, :L] = (bv * p_end).astype(BF16)
    bke_s[wr, b, L:] = (k2 * p_end).astype(BF16)
    vb_s[wr, b] = v.astype(BF16)
    bonus_s[wr, b] = bsum * v


def _rwkv_chain(src, rd, state_ref, rowp_ref, bd_ref, yg_ref):
    ar_s, bt_s, kt_s, bke_s, vb_s, bonus_s, g_s, plast_s = src
    L = CHUNK
    rows = yg_ref.shape[0]
    pairs = RWKV_DIM // LANES
    inst = [(b, p) for b in range(rows) for p in range(pairs)]
    cs = [slice(p * LANES, (p + 1) * LANES) for p in range(pairs)]
    n = range(len(inst))
    bd = bd_ref[...]
    lng, lnb = rowp_ref[5:6, :], rowp_ref[6:7, :]
    ti = lax.broadcasted_iota(jnp.int32, (2 * L, 2 * L), 0)
    tj = lax.broadcasted_iota(jnp.int32, (2 * L, 2 * L), 1)
    strict = ti > tj
    incl = ti >= tj
    eye = (ti == tj).astype(F32)
    same_head = (ti < L) == (tj < L)
    head_a = lax.broadcasted_iota(jnp.int32, (L, LANES), 1) < RWKV_HEAD_DIM

    def expand(x):
        zero = jnp.zeros_like(x)
        return jnp.concatenate([jnp.where(head_a, x, zero), jnp.where(head_a, zero, x)], axis=0)

    def fold(x):
        return x[:L] + x[L:]

    ar = [ar_s[rd, b, :, cs[p]] for b, p in inst]
    xa = [expand(t[:L]) for t in ar]
    xr = [expand(t[L:]) for t in ar]
    yb = [expand(bt_s[rd, b, :, cs[p]]) for b, p in inst]
    yk = [expand(kt_s[rd, b, :, cs[p]]) for b, p in inst]
    vb = [vb_s[rd, b, :, cs[p]] for b, p in inst]
    vm = [expand(t) for t in vb]
    s_old = [state_ref[i] for i in n]
    sob = [t.astype(BF16) for t in s_old]

    a_ab = [jnp.where(strict, _dot_nt(xa[i], yb[i]), 0.0) for i in n]
    a_ak = [jnp.where(strict, _dot_nt(xa[i], yk[i]), 0.0).astype(BF16) for i in n]
    a_rb = [jnp.where(incl, _dot_nt(xr[i], yb[i]), 0.0).astype(BF16) for i in n]
    a_rk = [jnp.where(incl, _dot_nt(xr[i], yk[i]), 0.0).astype(BF16) for i in n]
    ars = [_dot_nt(ar[i], sob[i]) for i in n]
    akv = [fold(_dot(a_ak[i], vm[i])) for i in n]
    rkv = [fold(_dot(a_rk[i], vm[i])) for i in n]
    yield

    t_inv = [eye + a_ab[i] for i in n]
    ab = [t.astype(BF16) for t in a_ab]
    pk = [_dot(ab[i], ab[i]) for i in n]
    for step in range(5):
        pkb = [t.astype(BF16) for t in pk]
        t_inv = [t_inv[i] + _dot(t_inv[i].astype(BF16), pkb[i]) for i in n]
        if step < 4:
            pk = [_dot(pkb[i], pkb[i]) for i in n]
        yield

    zm = [expand((ars[i][:L] + akv[i]).astype(BF16)) for i in n]
    u_st = [_dot(t_inv[i].astype(BF16), zm[i]).astype(BF16) for i in n]
    y = [ars[i][L:] + rkv[i] + fold(_dot(a_rb[i], u_st[i])) for i in n]
    uv = [jnp.concatenate([fold(u_st[i]), vb[i]], axis=0) for i in n]
    upd = [_dot_tn(uv[i], bke_s[rd, b, :, cs[p]]) for i, (b, p) in enumerate(inst)]
    for i, (b, p) in enumerate(inst):
        state_ref[i] = jnp.where(same_head, s_old[i] * plast_s[rd, b][:, cs[p]] + upd[i], 0.0)

    inv_n = 1.0 / RWKV_HEAD_DIM
    for i, (b, p) in enumerate(inst):
        yc = y[i] - _seg_dot(_split(y[i]), bd) * inv_n
        var = _seg_dot(_split(yc * yc), bd) * inv_n
        yn = yc * lax.rsqrt(var + LN_X_EPS) * lng[:, cs[p]] + lnb[:, cs[p]]
        yg_ref[b, :, cs[p]] = ((yn + bonus_s[rd, b, :, cs[p]]) * g_s[rd, b, :, cs[p]]).astype(BF16)


def _rwkv_body(rw_ref, shift_ref, s0_ref, rowp_ref, lora_ref, bd_ref, yg_ref, sfin_ref, state_ref, prev_ref, *slots):
    ci = pl.program_id(1)
    rows = rw_ref.shape[0]
    pairs = RWKV_DIM // LANES

    @pl.when(ci == 0)
    def _():
        for b in range(rows):
            state_ref[b * pairs:(b + 1) * pairs] = s0_ref[...]
            prev_ref[b] = shift_ref[1:2, :]
        for ref in slots[:-1]:
            ref[0] = jnp.zeros(ref.shape[1:], ref.dtype)
        slots[-1][0] = jnp.ones(slots[-1].shape[1:], F32)

    def step(rd, wr):
        chain = _rwkv_chain(slots, rd, state_ref, rowp_ref, bd_ref, yg_ref)
        preps = [_rwkv_prep(rw_ref, prev_ref, shift_ref, rowp_ref, lora_ref, bd_ref, slots, wr, b)
                 for b in range(rows)]
        for g in preps:
            next(g)
        next(chain)
        for g in preps:
            next(g)
        for g in preps:
            next(g)
        next(chain)
        next(chain)
        for g in preps:
            next(g)
        for g in preps:
            next(g, None)
        next(chain)
        next(chain)
        next(chain)
        next(chain, None)

    @pl.when(ci % 2 == 0)
    def _():
        step(0, 1)

    @pl.when(ci % 2 == 1)
    def _():
        step(1, 0)

    @pl.when(ci == pl.num_programs(1) - 1)
    def _():
        sfin_ref[...] = state_ref[...]


def _rwkv(rw, cprev0, s0, p):
    b, s, _ = rw.shape
    nc = s // CHUNK
    L = CHUNK
    pairs = RWKV_DIM // LANES
    rows = RWKV_ROWS if b % RWKV_ROWS == 0 else 1
    shift = jnp.concatenate([p["mu"], cprev0], axis=0)
    lane_head = jnp.arange(LANES) // RWKV_HEAD_DIM
    bd = (lane_head[:, None] == lane_head[None, :]).astype(BF16)
    full = lambda shape: pl.BlockSpec(shape, lambda i, c: (0,) * len(shape))
    return pl.pallas_call(
        _rwkv_body,
        grid=(b // rows, nc + 1),
        in_specs=[
            pl.BlockSpec((rows, CHUNK, RWKV_COLS), lambda i, c: (i, jnp.minimum(c, nc - 1), 0)),
            full(shift.shape),
            full((pairs, LANES, LANES)),
            full(p["rows"].shape),
            full(p["lora"].shape),
            full((LANES, LANES)),
        ],
        out_specs=[
            pl.BlockSpec((rows, CHUNK, RWKV_DIM), lambda i, c: (i, jnp.maximum(c - 1, 0), 0)),
            pl.BlockSpec((rows * pairs, LANES, LANES), lambda i, c: (i, 0, 0)),
        ],
        out_shape=[
            jax.ShapeDtypeStruct((b, s, RWKV_DIM), BF16),
            jax.ShapeDtypeStruct((b * pairs, LANES, LANES), F32),
        ],
        scratch_shapes=[
            pltpu.VMEM((rows * pairs, LANES, LANES), F32),
            pltpu.VMEM((rows, 1, RWKV_COLS), F32),
            pltpu.VMEM((2, rows, 2 * L, RWKV_DIM), BF16),
            pltpu.VMEM((2, rows, L, RWKV_DIM), BF16),
            pltpu.VMEM((2, rows, L, RWKV_DIM), BF16),
            pltpu.VMEM((2, rows, 2 * L, RWKV_DIM), BF16),
            pltpu.VMEM((2, rows, L, RWKV_DIM), BF16),
            pltpu.VMEM((2, rows, L, RWKV_DIM), F32),
            pltpu.VMEM((2, rows, L, RWKV_DIM), F32),
            pltpu.VMEM((2, rows, 1, RWKV_DIM), F32),
        ],
        compiler_params=_cparams("arbitrary", "arbitrary"),
        name="rwkv",
    )(rw, shift, s0, p["rows"], p["lora"], bd)


def _merge_body(x_ref, oa_ref, yg_ref, gate_ref, woa_ref, wor_ref, wout_ref, g2_ref, wrh_ref, wrl_ref, br_ref,
                h1_ref, xn_ref, info_ref, cnt_ref, base_ref):
    tm = x_ref.shape[0]

    @pl.when(pl.program_id(0) == 0)
    def _():
        base_ref[...] = jnp.zeros_like(base_ref)

    att = _dot(oa_ref[...], woa_ref[...])
    rwk = _dot(yg_ref[...], wor_ref[...])
    gate = gate_ref[...].astype(F32)
    merged = _sigmoid(gate[:, :D_MODEL]) * att + _sigmoid(gate[:, D_MODEL:]) * rwk
    h1 = x_ref[...] + _dot(merged.astype(BF16), wout_ref[...])
    h1_ref[...] = h1
    xn = h1 * lax.rsqrt(jnp.mean(h1 * h1, axis=-1, keepdims=True) + NORM_EPS) * g2_ref[...]
    xn_ref[...] = xn

    xn_hi = xn.astype(BF16)
    xn_lo = (xn - xn_hi.astype(F32)).astype(BF16)
    logits = (_dot(xn_hi, wrh_ref[...]) + _dot(xn_lo, wrh_ref[...]) + _dot(xn_hi, wrl_ref[...])) + br_ref[...]
    lane = lax.broadcasted_iota(jnp.int32, (tm, LANES), 1)
    neg_inf = jnp.float32(-jnp.inf)
    big = jnp.int32(LANES)
    gl = jnp.where(lane < N_GROUPS, logits, neg_inf)
    gmax = jnp.max(gl, axis=-1, keepdims=True)
    gsel = jnp.min(jnp.where(gl == gmax, lane, big), axis=-1, keepdims=True)
    p_sel = 1.0 / jnp.sum(jnp.exp(gl - gmax), axis=-1, keepdims=True)
    lo = ROUTER_LANE0 + gsel * EXPERTS_PER_GROUP
    el = jnp.where(jnp.logical_and(lane >= lo, lane < lo + EXPERTS_PER_GROUP), logits, neg_inf)
    v1 = jnp.max(el, axis=-1, keepdims=True)
    i1 = jnp.min(jnp.where(el == v1, lane, big), axis=-1, keepdims=True)
    el2 = jnp.where(lane == i1, neg_inf, el)
    v2 = jnp.max(el2, axis=-1, keepdims=True)
    i2 = jnp.min(jnp.where(el2 == v2, lane, big), axis=-1, keepdims=True)
    e2 = jnp.exp(v2 - v1)
    w1 = p_sel / (1.0 + e2)
    w2 = p_sel * e2 / (1.0 + e2)

    oh1 = (lane == i1).astype(F32)
    oh2 = (lane == i2).astype(F32)
    both = oh1 + oh2
    ti = lax.broadcasted_iota(jnp.int32, (tm, tm), 0)
    tj = lax.broadcasted_iota(jnp.int32, (tm, tm), 1)
    before = _dot((ti > tj).astype(BF16), both.astype(BF16)) + base_ref[...]
    rank1 = jnp.sum(before * oh1, axis=-1, keepdims=True)
    rank2 = jnp.sum(before * oh2, axis=-1, keepdims=True)
    base = base_ref[...] + jnp.sum(both, axis=0, keepdims=True)
    base_ref[...] = base
    cnt_ref[...] = base

    info = jnp.where(lane == 0, (i1 - ROUTER_LANE0).astype(F32), 0.0)
    info = jnp.where(lane == 1, (i2 - ROUTER_LANE0).astype(F32), info)
    info = jnp.where(lane == 2, w1, info)
    info = jnp.where(lane == 3, w2, info)
    info = jnp.where(lane == 4, rank1, info)
    info = jnp.where(lane == 5, rank2, info)
    info_ref[...] = info[:, :INFO_COLS]


def _merge(x2d, oa, yg, gate, woa, wor, wout, g2, wr, br, tm):
    wr_hi = wr.astype(BF16)
    wr_lo = (wr - wr_hi.astype(F32)).astype(BF16)
    m = x2d.shape[0]
    tile = lambda n: pl.BlockSpec((tm, n), lambda i: (i, 0))
    full = lambda a, b: pl.BlockSpec((a, b), lambda i: (0, 0))
    return pl.pallas_call(
        _merge_body,
        grid=(m // tm,),
        in_specs=[
            tile(D_MODEL), tile(ATT_Q), tile(RWKV_DIM), tile(GATE_COLS),
            full(ATT_Q, D_MODEL), full(RWKV_DIM, D_MODEL), full(D_MODEL, D_MODEL),
            full(1, D_MODEL), full(D_MODEL, LANES), full(D_MODEL, LANES), full(1, LANES),
        ],
        out_specs=[tile(D_MODEL), tile(D_MODEL), tile(INFO_COLS), full(1, LANES)],
        out_shape=[
            jax.ShapeDtypeStruct((m, D_MODEL), F32),
            jax.ShapeDtypeStruct((m, D_MODEL), F32),
            jax.ShapeDtypeStruct((m, INFO_COLS), F32),
            jax.ShapeDtypeStruct((1, LANES), F32),
        ],
        scratch_shapes=[pltpu.VMEM((1, LANES), F32)],
        compiler_params=_cparams("arbitrary"),
        name="merge",
    )(x2d, oa, yg, gate, woa, wor, wout, g2, wr_hi, wr_lo, br)


def _dispatch_body(max_trailing, pends_ref, dest_ref, xn_ref, xs_ref, zero_ref, sem, zsem):
    groups = xn_ref.shape[0]
    cap_blocks = xs_ref.shape[0] // SLOT_BLOCK

    @pl.when(pl.program_id(0) == 0)
    def _():
        zero_ref[...] = jnp.zeros_like(zero_ref)

        def zero_block(start):
            dst = xs_ref.at[pl.ds(pl.multiple_of(start, SLOT_BLOCK), SLOT_BLOCK)]
            return pltpu.make_async_copy(zero_ref, dst, zsem)

        ends = [pends_ref[e] for e in range(N_EXPERTS)]
        non_empty = [ends[e] > (ends[e - 1] if e else 0) for e in range(N_EXPERTS)]
        used_blocks = ends[-1] // SLOT_BLOCK
        trailing = [used_blocks + j < cap_blocks for j in range(max_trailing)]
        for e in range(N_EXPERTS):
            @pl.when(non_empty[e])
            def _():
                zero_block(ends[e] - SLOT_BLOCK).start()
        for j in range(max_trailing):
            @pl.when(trailing[j])
            def _():
                zero_block((used_blocks + j) * SLOT_BLOCK).start()
        for cond in non_empty + trailing:
            @pl.when(cond)
            def _():
                zero_block(0).wait()

    def row_copy(grp, sub, d):
        return pltpu.make_async_copy(xn_ref.at[grp, pl.ds(sub, 1)], xs_ref.at[pl.ds(d, 1)], sem)

    def start(grp, carry):
        for sub in range(SUBLANES):
            for k in range(TOP_K):
                row_copy(grp, sub, dest_ref[0, 0, grp * (SUBLANES * TOP_K) + sub * TOP_K + k]).start()
        return carry

    def wait(grp, carry):
        for _ in range(SUBLANES * TOP_K):
            row_copy(0, 0, 0).wait()
        return carry

    lax.fori_loop(0, groups, start, 0)
    lax.fori_loop(0, groups, wait, 0)


def _dispatch(pends, dest3, xn, cap):
    tm = dest3.shape[2] // TOP_K
    xn = xn.reshape(xn.shape[0] // SUBLANES, SUBLANES, D_MODEL)
    max_trailing = cap // SLOT_BLOCK - (dest3.shape[0] * dest3.shape[2]) // SLOT_BLOCK
    return pl.pallas_call(
        functools.partial(_dispatch_body, max_trailing),
        grid_spec=pltpu.PrefetchScalarGridSpec(
            num_scalar_prefetch=1,
            grid=(dest3.shape[0],),
            in_specs=[
                pl.BlockSpec((1, 1, dest3.shape[2]), lambda i, pe: (i, 0, 0), memory_space=pltpu.SMEM),
                pl.BlockSpec((tm // SUBLANES, SUBLANES, D_MODEL), lambda i, pe: (i, 0, 0)),
            ],
            out_specs=pl.BlockSpec(memory_space=pl.ANY),
            scratch_shapes=[pltpu.VMEM((SLOT_BLOCK, D_MODEL), F32), pltpu.SemaphoreType.DMA(()),
                            pltpu.SemaphoreType.DMA(())],
        ),
        out_shape=jax.ShapeDtypeStruct((cap, D_MODEL), F32),
        compiler_params=_cparams("arbitrary"),
        name="dispatch",
    )(pends, dest3, xn)


def _expert_body(be_ref, xs_ref, wg_ref, wu_ref, wd_ref, yb_ref):
    del be_ref
    xb = xs_ref[...].astype(BF16)
    gt = _dot(xb, wg_ref[...])
    up = _dot(xb, wu_ref[...])
    hmid = gt * _sigmoid(gt) * up
    yb_ref[...] = _dot(hmid.astype(BF16), wd_ref[...])


def _experts(block_expert, xs, wg, wu, wd):
    cap = xs.shape[0]
    nblk = cap // SLOT_BLOCK
    return pl.pallas_call(
        _expert_body,
        grid_spec=pltpu.PrefetchScalarGridSpec(
            num_scalar_prefetch=1,
            grid=(nblk,),
            in_specs=[
                pl.BlockSpec((SLOT_BLOCK, D_MODEL), lambda i, be: (i, 0)),
                pl.BlockSpec((None, D_MODEL, D_EXPERT), lambda i, be: (be[i], 0, 0)),
                pl.BlockSpec((None, D_MODEL, D_EXPERT), lambda i, be: (be[i], 0, 0)),
                pl.BlockSpec((None, D_EXPERT, D_MODEL), lambda i, be: (be[i], 0, 0)),
            ],
            out_specs=pl.BlockSpec((SLOT_BLOCK, D_MODEL), lambda i, be: (i, 0)),
        ),
        out_shape=jax.ShapeDtypeStruct((cap, D_MODEL), F32),
        compiler_params=_cparams("arbitrary"),
        name="experts",
    )(block_expert, xs, wg, wu, wd)


def _combine_body(dcur_ref, dnext_ref, h1_ref, info_ref, gf_ref, yb_ref, out_ref, buf_ref, sem):
    i = pl.program_id(0)
    tm = h1_ref.shape[0]
    groups = tm // SUBLANES

    def row_copy(d, slot, k, grp, sub):
        return pltpu.make_async_copy(yb_ref.at[pl.ds(d, 1)], buf_ref.at[slot, k, grp, pl.ds(sub, 1)], sem.at[slot])

    def issue(dref, slot):
        def start(grp, carry):
            for sub in range(SUBLANES):
                for k in range(TOP_K):
                    row_copy(dref[0, 0, grp * (SUBLANES * TOP_K) + sub * TOP_K + k], slot, k, grp, sub).start()
            return carry
        lax.fori_loop(0, groups, start, 0)

    def step(slot):
        @pl.when(i == 0)
        def _():
            issue(dcur_ref, slot)

        @pl.when(i + 1 < pl.num_programs(0))
        def _():
            issue(dnext_ref, 1 - slot)

        def wait(grp, carry):
            for _ in range(SUBLANES * TOP_K):
                row_copy(0, slot, 0, 0, 0).wait()
            return carry
        lax.fori_loop(0, groups, wait, 0)
        info = info_ref[...]
        y0 = buf_ref[slot, 0].reshape(tm, D_MODEL)
        y1 = buf_ref[slot, 1].reshape(tm, D_MODEL)
        y = y0 * info[:, 2:3] + y1 * info[:, 3:4]
        h2 = h1_ref[...] + y
        out_ref[...] = h2 * lax.rsqrt(jnp.mean(h2 * h2, axis=-1, keepdims=True) + NORM_EPS) * gf_ref[...]

    @pl.when(i % 2 == 0)
    def _():
        step(0)

    @pl.when(i % 2 == 1)
    def _():
        step(1)


def _combine(dest3, h1, info, gf, yb):
    m = h1.shape[0]
    nt = dest3.shape[0]
    tm = m // nt
    dspec = lambda f: pl.BlockSpec((1, 1, dest3.shape[2]), f, memory_space=pltpu.SMEM)
    return pl.pallas_call(
        _combine_body,
        grid=(nt,),
        in_specs=[
            dspec(lambda i: (i, 0, 0)),
            dspec(lambda i: (jnp.minimum(i + 1, nt - 1), 0, 0)),
            pl.BlockSpec((tm, D_MODEL), lambda i: (i, 0)),
            pl.BlockSpec((tm, INFO_COLS), lambda i: (i, 0)),
            pl.BlockSpec((1, D_MODEL), lambda i: (0, 0)),
            pl.BlockSpec(memory_space=pl.ANY),
        ],
        out_specs=pl.BlockSpec((tm, D_MODEL), lambda i: (i, 0)),
        out_shape=jax.ShapeDtypeStruct((m, D_MODEL), F32),
        scratch_shapes=[pltpu.VMEM((2, TOP_K, tm // SUBLANES, SUBLANES, D_MODEL), F32),
                        pltpu.SemaphoreType.DMA((2,))],
        compiler_params=_cparams("arbitrary"),
        name="combine",
    )(dest3, dest3, h1, info, gf, yb)


def _tile(m, pref):
    t = pref
    while m % t:
        t //= 2
    return t


def kernel(x, meta_tokens, norm_mix_g, w_in, shift_mu, w0, w_decay_up, a0, w_aaa_up, w_gate_up, k_k, k_a, r_k,
           ln_x_g, ln_x_b, attn_sinks, w_o_attn, w_o_rwkv, w_out, norm_ffn_g, w_grp, b_grp, w_exp, b_exp,
           e_gate, e_up, e_down, norm_final_g):
    assert norm_mix_g.shape[0] == 1, "single-layer trunk"
    b, s, d = x.shape
    assert d == D_MODEL and s % BLOCK == 0
    m = b * s
    row = lambda t: t.reshape(1, -1).astype(F32)
    w_in_b = w_in[0].astype(BF16)
    g_mix = row(norm_mix_g[0])
    rp = dict(mu=row(shift_mu[0]),
              rows=jnp.concatenate([row(t[0]) for t in (w0, a0, k_k, k_a, r_k, ln_x_g, ln_x_b)]
                                   + [jnp.zeros((1, RWKV_DIM), F32)], axis=0),
              lora=jnp.concatenate([w_decay_up[0], w_aaa_up[0], w_gate_up[0]], axis=0).astype(BF16))

    x0 = jnp.concatenate([jnp.zeros((BLOCK - N_META, d), F32), meta_tokens.astype(F32)], axis=0)
    qkv0, rw0, _ = _proj(x0, g_mix, w_in_b, BLOCK)
    kmeta = qkv0[BLOCK - N_META:, ATT_Q:ATT_Q + ATT_KV]
    vmeta = qkv0[BLOCK - N_META:, ATT_Q + ATT_KV:]
    zero_state = jnp.zeros((RWKV_DIM // LANES, LANES, LANES), F32)
    _, s_meta = _rwkv(rw0[None], jnp.zeros((1, RWKV_COLS), F32), zero_state, rp)

    x2d = x.reshape(m, d)
    qkv, rw, gate = _proj(x2d, g_mix, w_in_b, _tile(m, ROW_TILE))
    o_att = _attention(qkv.reshape(b, s, ATT_COLS), kmeta, vmeta, attn_sinks[0].astype(F32))
    yg, _ = _rwkv(rw.reshape(b, s, RWKV_COLS), rw0[BLOCK - 1:], s_meta, rp)

    wr = jnp.zeros((d, LANES), F32).at[:, :N_GROUPS].set(w_grp[0]).at[:, N_GROUPS:N_GROUPS + N_EXPERTS].set(w_exp[0])
    br = jnp.zeros((1, LANES), F32).at[0, :N_GROUPS].set(b_grp[0]).at[0, N_GROUPS:N_GROUPS + N_EXPERTS].set(b_exp[0])
    tm = _tile(m, DMA_TILE)
    h1, xn2, info, cnt = _merge(x2d, o_att.reshape(m, ATT_Q), yg.reshape(m, RWKV_DIM), gate,
                                w_o_attn[0].astype(BF16), w_o_rwkv[0].astype(BF16), w_out[0].astype(BF16),
                                row(norm_ffn_g[0]), wr, br, _tile(m, ROW_TILE))

    counts = cnt[0, N_GROUPS:N_GROUPS + N_EXPERTS].astype(jnp.int32)
    padded = ((counts + SLOT_BLOCK - 1) // SLOT_BLOCK) * SLOT_BLOCK
    pends = jnp.cumsum(padded)
    pstarts = pends - padded
    eid = info[:, 0:2].astype(jnp.int32)
    dest = pstarts[eid] + info[:, 4:6].astype(jnp.int32)
    nblk = -(-(m * TOP_K + N_EXPERTS * (SLOT_BLOCK - 1)) // SLOT_BLOCK)
    cap = nblk * SLOT_BLOCK
    block_start = jnp.arange(nblk, dtype=jnp.int32) * SLOT_BLOCK
    block_expert = jnp.minimum(jnp.sum(block_start[:, None] >= pends[None, :], axis=1), N_EXPERTS - 1).astype(jnp.int32)
    dest3 = dest.reshape(m // tm, 1, 2 * tm)

    xs = _dispatch(pends.astype(jnp.int32), dest3, xn2, cap)
    yb = _experts(block_expert, xs, e_gate[0].astype(BF16), e_up[0].astype(BF16), e_down[0].astype(BF16))
    out = _combine(dest3, h1, info, row(norm_final_g), yb)
    return out.reshape(b, s, d)
```

```python
import functools

import numpy as np
import jax
import jax.numpy as jnp
from jax import lax
from jax.experimental import pallas as pl
from jax.experimental.pallas import tpu as pltpu

F32 = jnp.float32
BF16 = jnp.bfloat16

D_MODEL = 1024
N_META = 16
ATT_HEADS = 8
ATT_KV_HEADS = 2
ATT_GROUP = ATT_HEADS // ATT_KV_HEADS
HEAD_DIM = 64
WINDOW = 128
BLOCK = 128
RWKV_HEADS = 8
RWKV_HEAD_DIM = 64
RWKV_DIM = RWKV_HEADS * RWKV_HEAD_DIM
DECAY_LORA = 64
AAA_LORA = 64
GATE_LORA = 128
LN_X_EPS = 64e-5
ATT_Q = ATT_HEADS * HEAD_DIM
ATT_KV = ATT_KV_HEADS * HEAD_DIM
ATT_COLS = ATT_Q + 2 * ATT_KV
RWKV_COLS = 3 * RWKV_DIM + DECAY_LORA + AAA_LORA + GATE_LORA
GATE_COLS = 2 * D_MODEL
IN_COLS = ATT_COLS + RWKV_COLS + GATE_COLS
N_GROUPS = 4
EXPERTS_PER_GROUP = 8
N_EXPERTS = N_GROUPS * EXPERTS_PER_GROUP
TOP_K = 2
D_EXPERT = 256
NORM_EPS = 1e-6

LANES = 128
SUBLANES = 8
CHUNK = 64
RWKV_ROWS = 4
SLOT_BLOCK = 512
ROW_TILE = 512
DMA_TILE = 512
DMA_UNROLL = 8
ROUTER_LANE0 = N_GROUPS
INFO_COLS = 8
VMEM_LIMIT = 56 * 1024 * 1024


def _cparams(*sem):
    return pltpu.CompilerParams(dimension_semantics=sem, vmem_limit_bytes=VMEM_LIMIT)


def _dot(a, b):
    return jnp.dot(a, b, preferred_element_type=F32)


def _dot_nt(a, b):
    return lax.dot_general(a, b, (((1,), (1,)), ((), ())), preferred_element_type=F32)


def _dot_tn(a, b):
    return lax.dot_general(a, b, (((0,), (0,)), ((), ())), preferred_element_type=F32)


def _sigmoid(x):
    return 0.5 * jnp.tanh(0.5 * x) + 0.5


def _proj_body(x_ref, g_ref, w_ref, qkv_ref, rw_ref, gate_ref):
    x = x_ref[...]
    y = x * lax.rsqrt(jnp.mean(x * x, axis=-1, keepdims=True) + NORM_EPS)
    xb = (y * g_ref[...]).astype(BF16)
    qkv_ref[...] = _dot(xb, w_ref[:, :ATT_COLS]).astype(BF16)
    rw_ref[...] = _dot(xb, w_ref[:, ATT_COLS:ATT_COLS + RWKV_COLS])
    gate_ref[...] = _dot(xb, w_ref[:, ATT_COLS + RWKV_COLS:]).astype(BF16)


def _proj(x2d, g, w_bf16, tm):
    m = x2d.shape[0]
    return pl.pallas_call(
        _proj_body,
        grid=(m // tm,),
        in_specs=[
            pl.BlockSpec((tm, D_MODEL), lambda i: (i, 0)),
            pl.BlockSpec((1, D_MODEL), lambda i: (0, 0)),
            pl.BlockSpec((D_MODEL, IN_COLS), lambda i: (0, 0)),
        ],
        out_specs=[
            pl.BlockSpec((tm, ATT_COLS), lambda i: (i, 0)),
            pl.BlockSpec((tm, RWKV_COLS), lambda i: (i, 0)),
            pl.BlockSpec((tm, GATE_COLS), lambda i: (i, 0)),
        ],
        out_shape=[
            jax.ShapeDtypeStruct((m, ATT_COLS), BF16),
            jax.ShapeDtypeStruct((m, RWKV_COLS), F32),
            jax.ShapeDtypeStruct((m, GATE_COLS), BF16),
        ],
        compiler_params=_cparams("parallel"),
        name="proj",
    )(x2d, g, w_bf16)


def _attn_body(sink_ref, q_ref, kc_ref, kp_ref, vc_ref, vp_ref, km_ref, vm_ref, o_ref):
    n = pl.program_id(1)
    rows = ATT_GROUP * BLOCK
    ri = lax.broadcasted_iota(jnp.int32, (rows, BLOCK), 0)
    kj = lax.broadcasted_iota(jnp.int32, (rows, BLOCK), 1)
    d_cur = (ri % BLOCK) - kj
    valid_cur = d_cur >= 0
    valid_prev = jnp.logical_and(d_cur < 0, n > 0)
    f_cur = d_cur.astype(F32)
    f_prev = f_cur + float(BLOCK)
    rm = lax.broadcasted_iota(jnp.int32, (rows, N_META), 0)
    mi = lax.broadcasted_iota(jnp.int32, (rows, N_META), 1)
    f_meta = jnp.minimum(N_META + n * BLOCK + (rm % BLOCK) - mi, WINDOW).astype(F32)
    head_in_group = lax.broadcasted_iota(jnp.int32, (rows, 1), 0) // BLOCK
    scale = HEAD_DIM ** -0.5
    neg_inf = jnp.float32(-jnp.inf)
    groups = range(ATT_KV_HEADS)
    cols = [slice(g * HEAD_DIM, (g + 1) * HEAD_DIM) for g in groups]

    slope, sink = [], []
    for g in groups:
        sl_g = jnp.zeros((rows, 1), F32)
        sk_g = jnp.zeros((rows, 1), F32)
        for j in range(ATT_GROUP):
            h = g * ATT_GROUP + j
            sl_g = jnp.where(head_in_group == j, float(np.float32(2.0 ** (-8.0 * (h + 1) / ATT_HEADS))), sl_g)
            sk_g = jnp.where(head_in_group == j, sink_ref[h], sk_g)
        slope.append(sl_g)
        sink.append(sk_g)

    q4 = [jnp.concatenate([q_ref[:, (g * ATT_GROUP + j) * HEAD_DIM:(g * ATT_GROUP + j + 1) * HEAD_DIM]
                           for j in range(ATT_GROUP)], axis=0) for g in groups]
    q4 = [t * jnp.asarray(scale, BF16) for t in q4]
    s_c = [_dot_nt(q4[g], kc_ref[:, cols[g]]) for g in groups]
    s_p = [_dot_nt(q4[g], kp_ref[:, cols[g]]) for g in groups]
    s_m = [_dot_nt(q4[g], km_ref[:, cols[g]]) for g in groups]
    l_c = [jnp.where(valid_cur, s_c[g] - slope[g] * f_cur, neg_inf) for g in groups]
    l_p = [jnp.where(valid_prev, s_p[g] - slope[g] * f_prev, neg_inf) for g in groups]
    l_m = [s_m[g] - slope[g] * f_meta for g in groups]
    m = [jnp.maximum(jnp.maximum(jnp.max(jnp.maximum(l_c[g], l_p[g]), axis=-1, keepdims=True),
                                 jnp.max(l_m[g], axis=-1, keepdims=True)), sink[g]) for g in groups]
    p_c = [jnp.exp(l_c[g] - m[g]) for g in groups]
    p_p = [jnp.exp(l_p[g] - m[g]) for g in groups]
    p_m = [jnp.exp(l_m[g] - m[g]) for g in groups]
    inv = [1.0 / (jnp.sum(p_c[g] + p_p[g], axis=-1, keepdims=True) + jnp.sum(p_m[g], axis=-1, keepdims=True)
                  + jnp.exp(sink[g] - m[g])) for g in groups]
    o4 = [(_dot((p_c[g] * inv[g]).astype(BF16), vc_ref[:, cols[g]])
           + _dot((p_p[g] * inv[g]).astype(BF16), vp_ref[:, cols[g]])
           + _dot((p_m[g] * inv[g]).astype(BF16), vm_ref[:, cols[g]])).astype(BF16) for g in groups]
    o_ref[...] = jnp.concatenate([o4[g][j * BLOCK:(j + 1) * BLOCK] for g in groups for j in range(ATT_GROUP)], axis=1)


def _attention(qkv, kmeta, vmeta, sinks):
    b, s, _ = qkv.shape
    nb = s // BLOCK
    kcol, vcol = ATT_Q // ATT_KV, ATT_Q // ATT_KV + 1
    prev = lambda n: jnp.maximum(n - 1, 0)
    return pl.pallas_call(
        _attn_body,
        grid=(b, nb),
        in_specs=[
            pl.BlockSpec(memory_space=pltpu.SMEM),
            pl.BlockSpec((None, BLOCK, ATT_Q), lambda i, n: (i, n, 0)),
            pl.BlockSpec((None, BLOCK, ATT_KV), lambda i, n: (i, n, kcol)),
            pl.BlockSpec((None, BLOCK, ATT_KV), lambda i, n: (i, prev(n), kcol)),
            pl.BlockSpec((None, BLOCK, ATT_KV), lambda i, n: (i, n, vcol)),
            pl.BlockSpec((None, BLOCK, ATT_KV), lambda i, n: (i, prev(n), vcol)),
            pl.BlockSpec((N_META, ATT_KV), lambda i, n: (0, 0)),
            pl.BlockSpec((N_META, ATT_KV), lambda i, n: (0, 0)),
        ],
        out_specs=pl.BlockSpec((None, BLOCK, ATT_Q), lambda i, n: (i, n, 0)),
        out_shape=jax.ShapeDtypeStruct((b, s, ATT_Q), BF16),
        compiler_params=_cparams("parallel", "parallel"),
        name="attn",
    )(sinks, qkv, qkv, qkv, qkv, qkv, kmeta, vmeta)


def _split(x):
    hi = x.astype(BF16)
    return hi, (x - hi.astype(F32)).astype(BF16)


def _seg_dot(parts, bd):
    hi, lo = parts
    cols = range(hi.shape[1] // LANES)
    return jnp.concatenate([_dot(hi[:, p * LANES:(p + 1) * LANES], bd) + _dot(lo[:, p * LANES:(p + 1) * LANES], bd)
                            for p in cols], axis=1)


def _rwkv_prep(rw_ref, prev_ref, shift_ref, rowp_ref, lora_ref, bd_ref, dst, wr, b):
    ar_s, bt_s, kt_s, bke_s, vb_s, bonus_s, g_s, plast_s = dst
    L = CHUNK
    w0, a0, kk, ka, rk = (rowp_ref[i:i + 1, :] for i in range(5))
    wdec = lora_ref[:DECAY_LORA, :]
    waaa = lora_ref[DECAY_LORA:DECAY_LORA + AAA_LORA, :]
    wgate = lora_ref[DECAY_LORA + AAA_LORA:, :]
    bd = bd_ref[...]
    c = rw_ref[b]
    row = lax.broadcasted_iota(jnp.int32, (L, 1), 0)
    cp = jnp.where(row == 0, prev_ref[b], pltpu.roll(c, 1, 0))
    prev_ref[b] = c[L - 1:L, :]
    x = c + shift_ref[0:1, :] * (cp - c)
    r = x[:, :RWKV_DIM]
    k = x[:, RWKV_DIM:2 * RWKV_DIM]
    v = x[:, 2 * RWKV_DIM:3 * RWKV_DIM]
    o = 3 * RWKV_DIM
    tw = jnp.tanh(x[:, o:o + DECAY_LORA]).astype(BF16)
    xa = x[:, o + DECAY_LORA:o + DECAY_LORA + AAA_LORA].astype(BF16)
    sg = _sigmoid(x[:, o + DECAY_LORA + AAA_LORA:]).astype(BF16)
    yield
    z = -(w0 + _dot(tw, wdec))
    a = _sigmoid(a0 + _dot(xa, waaa))
    g_s[wr, b] = _dot(sg, wgate)
    yield
    softplus = jnp.maximum(z, 0.0) + jnp.log(1.0 + jnp.exp(-jnp.abs(z)))
    lw = -jnp.exp(-softplus - 0.5)
    kkr = k * kk
    k2 = k * (1.0 + (a - 1.0) * ka)

    cum = lw
    shift = 1
    while shift < L:
        cum = cum + jnp.where(row >= shift, pltpu.roll(cum, shift, 0), 0.0)
        shift *= 2
    cum_last = cum[L - 1:L, :]
    p_inc = jnp.exp(cum)
    p_inv = jnp.exp(-cum)
    p_prev = jnp.exp(cum - lw)
    p_end = jnp.exp(cum_last - cum)
    plast_s[wr, b] = jnp.exp(cum_last)
    sq_parts = _split(kkr * kkr)
    bonus_parts = _split(r * k2 * rk)
    yield
    ssq = _seg_dot(sq_parts, bd)
    bsum = _seg_dot(bonus_parts, bd)
    yield
    kkn = kkr / jnp.maximum(jnp.sqrt(ssq), 1e-12)
    bv = kkn * a
    ar_s[wr, b, :L] = (-kkn * p_prev).astype(BF16)
    ar_s[wr, b, L:] = (r * p_inc).astype(BF16)
    bt_s[wr, b] = (bv * p_inv).astype(BF16)
    kt_s[wr, b] = (k2 * p_inv).astype(BF16)
    bke_s[wr, b, :L] = (bv * p_end).astype(BF16)
    bke_s[wr, b, L:] = (k2 * p_end).astype(BF16)
    vb_s[wr, b] = v.astype(BF16)
    bonus_s[wr, b] = bsum * v


def _rwkv_chain(src, rd, state_ref, rowp_ref, bd_ref, yg_ref):
    ar_s, bt_s, kt_s, bke_s, vb_s, bonus_s, g_s, plast_s = src
    L = CHUNK
    rows = yg_ref.shape[0]
    pairs = RWKV_DIM // LANES
    inst = [(b, p) for b in range(rows) for p in range(pairs)]
    cs = [slice(p * LANES, (p + 1) * LANES) for p in range(pairs)]
    n = range(len(inst))
    bd = bd_ref[...]
    lng, lnb = rowp_ref[5:6, :], rowp_ref[6:7, :]
    ti = lax.broadcasted_iota(jnp.int32, (2 * L, 2 * L), 0)
    tj = lax.broadcasted_iota(jnp.int32, (2 * L, 2 * L), 1)
    strict = ti > tj
    incl = ti >= tj
    eye = (ti == tj).astype(F32)
    same_head = (ti < L) == (tj < L)
    head_a = lax.broadcasted_iota(jnp.int32, (L, LANES), 1) < RWKV_HEAD_DIM

    def expand(x):
        zero = jnp.zeros_like(x)
        return jnp.concatenate([jnp.where(head_a, x, zero), jnp.where(head_a, zero, x)], axis=0)

    def fold(x):
        return x[:L] + x[L:]

    ar = [ar_s[rd, b, :, cs[p]] for b, p in inst]
    xa = [expand(t[:L]) for t in ar]
    xr = [expand(t[L:]) for t in ar]
    yb = [expand(bt_s[rd, b, :, cs[p]]) for b, p in inst]
    yk = [expand(kt_s[rd, b, :, cs[p]]) for b, p in inst]
    vb = [vb_s[rd, b, :, cs[p]] for b, p in inst]
    vm = [expand(t) for t in vb]
    s_old = [state_ref[i] for i in n]
    sob = [t.astype(BF16) for t in s_old]

    a_ab = [jnp.where(strict, _dot_nt(xa[i], yb[i]), 0.0) for i in n]
    a_ak = [jnp.where(strict, _dot_nt(xa[i], yk[i]), 0.0).astype(BF16) for i in n]
    a_rb = [jnp.where(incl, _dot_nt(xr[i], yb[i]), 0.0).astype(BF16) for i in n]
    a_rk = [jnp.where(incl, _dot_nt(xr[i], yk[i]), 0.0).astype(BF16) for i in n]
    ars = [_dot_nt(ar[i], sob[i]) for i in n]
    akv = [fold(_dot(a_ak[i], vm[i])) for i in n]
    rkv = [fold(_dot(a_rk[i], vm[i])) for i in n]
    yield

    t_inv = [eye + a_ab[i] for i in n]
    ab = [t.astype(BF16) for t in a_ab]
    pk = [_dot(ab[i], ab[i]) for i in n]
    for step in range(5):
        pkb = [t.astype(BF16) for t in pk]
        t_inv = [t_inv[i] + _dot(t_inv[i].astype(BF16), pkb[i]) for i in n]
        if step < 4:
            pk = [_dot(pkb[i], pkb[i]) for i in n]
        yield

    zm = [expand((ars[i][:L] + akv[i]).astype(BF16)) for i in n]
    u_st = [_dot(t_inv[i].astype(BF16), zm[i]).astype(BF16) for i in n]
    y = [ars[i][L:] + rkv[i] + fold(_dot(a_rb[i], u_st[i])) for i in n]
    uv = [jnp.concatenate([fold(u_st[i]), vb[i]], axis=0) for i in n]
    upd = [_dot_tn(uv[i], bke_s[rd, b, :, cs[p]]) for i, (b, p) in enumerate(inst)]
    for i, (b, p) in enumerate(inst):
        state_ref[i] = jnp.where(same_head, s_old[i] * plast_s[rd, b][:, cs[p]] + upd[i], 0.0)

    inv_n = 1.0 / RWKV_HEAD_DIM
    for i, (b, p) in enumerate(inst):
        yc = y[i] - _seg_dot(_split(y[i]), bd) * inv_n
        var = _seg_dot(_split(yc * yc), bd) * inv_n
        yn = yc * lax.rsqrt(var + LN_X_EPS) * lng[:, cs[p]] + lnb[:, cs[p]]
        yg_ref[b, :, cs[p]] = ((yn + bonus_s[rd, b, :, cs[p]]) * g_s[rd, b, :, cs[p]]).astype(BF16)


def _rwkv_body(rw_ref, shift_ref, s0_ref, rowp_ref, lora_ref, bd_ref, yg_ref, sfin_ref, state_ref, prev_ref, *slots):
    ci = pl.program_id(1)
    rows = rw_ref.shape[0]
    pairs = RWKV_DIM // LANES

    @pl.when(ci == 0)
    def _():
        for b in range(rows):
            state_ref[b * pairs:(b + 1) * pairs] = s0_ref[...]
            prev_ref[b] = shift_ref[1:2, :]
        for ref in slots[:-1]:
            ref[0] = jnp.zeros(ref.shape[1:], ref.dtype)
        slots[-1][0] = jnp.ones(slots[-1].shape[1:], F32)

    def step(rd, wr):
        chain = _rwkv_chain(slots, rd, state_ref, rowp_ref, bd_ref, yg_ref)
        preps = [_rwkv_prep(rw_ref, prev_ref, shift_ref, rowp_ref, lora_ref, bd_ref, slots, wr, b)
                 for b in range(rows)]
        for g in preps:
            next(g)
        next(chain)
        for g in preps:
            next(g)
        for g in preps:
            next(g)
        next(chain)
        next(chain)
        for g in preps:
            next(g)
        for g in preps:
            next(g, None)
        next(chain)
        next(chain)
        next(chain)
        next(chain, None)

    @pl.when(ci % 2 == 0)
    def _():
        step(0, 1)

    @pl.when(ci % 2 == 1)
    def _():
        step(1, 0)

    @pl.when(ci == pl.num_programs(1) - 1)
    def _():
        sfin_ref[...] = state_ref[...]


def _rwkv(rw, cprev0, s0, p):
    b, s, _ = rw.shape
    nc = s // CHUNK
    L = CHUNK
    pairs = RWKV_DIM // LANES
    rows = RWKV_ROWS if b % RWKV_ROWS == 0 else 1
    shift = jnp.concatenate([p["mu"], cprev0], axis=0)
    lane_head = jnp.arange(LANES) // RWKV_HEAD_DIM
    bd = (lane_head[:, None] == lane_head[None, :]).astype(BF16)
    full = lambda shape: pl.BlockSpec(shape, lambda i, c: (0,) * len(shape))
    return pl.pallas_call(
        _rwkv_body,
        grid=(b // rows, nc + 1),
        in_specs=[
            pl.BlockSpec((rows, CHUNK, RWKV_COLS), lambda i, c: (i, jnp.minimum(c, nc - 1), 0)),
            full(shift.shape),
            full((pairs, LANES, LANES)),
            full(p["rows"].shape),
            full(p["lora"].shape),
            full((LANES, LANES)),
        ],
        out_specs=[
            pl.BlockSpec((rows, CHUNK, RWKV_DIM), lambda i, c: (i, jnp.maximum(c - 1, 0), 0)),
            pl.BlockSpec((rows * pairs, LANES, LANES), lambda i, c: (i, 0, 0)),
        ],
        out_shape=[
            jax.ShapeDtypeStruct((b, s, RWKV_DIM), BF16),
            jax.ShapeDtypeStruct((b * pairs, LANES, LANES), F32),
        ],
        scratch_shapes=[
            pltpu.VMEM((rows * pairs, LANES, LANES), F32),
            pltpu.VMEM((rows, 1, RWKV_COLS), F32),
            pltpu.VMEM((2, rows, 2 * L, RWKV_DIM), BF16),
            pltpu.VMEM((2, rows, L, RWKV_DIM), BF16),
            pltpu.VMEM((2, rows, L, RWKV_DIM), BF16),
            pltpu.VMEM((2, rows, 2 * L, RWKV_DIM), BF16),
            pltpu.VMEM((2, rows, L, RWKV_DIM), BF16),
            pltpu.VMEM((2, rows, L, RWKV_DIM), F32),
            pltpu.VMEM((2, rows, L, RWKV_DIM), F32),
            pltpu.VMEM((2, rows, 1, RWKV_DIM), F32),
        ],
        compiler_params=_cparams("arbitrary", "arbitrary"),
        name="rwkv",
    )(rw, shift, s0, p["rows"], p["lora"], bd)


def _merge_body(x_ref, oa_ref, yg_ref, gate_ref, woa_ref, wor_ref, wout_ref, g2_ref, wrh_ref, wrl_ref, br_ref,
                h1_ref, xn_ref, info_ref, cnt_ref, base_ref):
    tm = x_ref.shape[0]

    @pl.when(pl.program_id(0) == 0)
    def _():
        base_ref[...] = jnp.zeros_like(base_ref)

    att = _dot(oa_ref[...], woa_ref[...])
    rwk = _dot(yg_ref[...], wor_ref[...])
    gate = gate_ref[...].astype(F32)
    merged = _sigmoid(gate[:, :D_MODEL]) * att + _sigmoid(gate[:, D_MODEL:]) * rwk
    h1 = x_ref[...] + _dot(merged.astype(BF16), wout_ref[...])
    h1_ref[...] = h1
    xn = h1 * lax.rsqrt(jnp.mean(h1 * h1, axis=-1, keepdims=True) + NORM_EPS) * g2_ref[...]
    xn_ref[...] = xn

    xn_hi = xn.astype(BF16)
    xn_lo = (xn - xn_hi.astype(F32)).astype(BF16)
    logits = (_dot(xn_hi, wrh_ref[...]) + _dot(xn_lo, wrh_ref[...]) + _dot(xn_hi, wrl_ref[...])) + br_ref[...]
    lane = lax.broadcasted_iota(jnp.int32, (tm, LANES), 1)
    neg_inf = jnp.float32(-jnp.inf)
    big = jnp.int32(LANES)
    gl = jnp.where(lane < N_GROUPS, logits, neg_inf)
    gmax = jnp.max(gl, axis=-1, keepdims=True)
    gsel = jnp.min(jnp.where(gl == gmax, lane, big), axis=-1, keepdims=True)
    p_sel = 1.0 / jnp.sum(jnp.exp(gl - gmax), axis=-1, keepdims=True)
    lo = ROUTER_LANE0 + gsel * EXPERTS_PER_GROUP
    el = jnp.where(jnp.logical_and(lane >= lo, lane < lo + EXPERTS_PER_GROUP), logits, neg_inf)
    v1 = jnp.max(el, axis=-1, keepdims=True)
    i1 = jnp.min(jnp.where(el == v1, lane, big), axis=-1, keepdims=True)
    el2 = jnp.where(lane == i1, neg_inf, el)
    v2 = jnp.max(el2, axis=-1, keepdims=True)
    i2 = jnp.min(jnp.where(el2 == v2, lane, big), axis=-1, keepdims=True)
    e2 = jnp.exp(v2 - v1)
    w1 = p_sel / (1.0 + e2)
    w2 = p_sel * e2 / (1.0 + e2)

    oh1 = (lane == i1).astype(F32)
    oh2 = (lane == i2).astype(F32)
    both = oh1 + oh2
    ti = lax.broadcasted_iota(jnp.int32, (tm, tm), 0)
    tj = lax.broadcasted_iota(jnp.int32, (tm, tm), 1)
    before = _dot((ti > tj).astype(BF16), both.astype(BF16)) + base_ref[...]
    rank1 = jnp.sum(before * oh1, axis=-1, keepdims=True)
    rank2 = jnp.sum(before * oh2, axis=-1, keepdims=True)
    base = base_ref[...] + jnp.sum(both, axis=0, keepdims=True)
    base_ref[...] = base
    cnt_ref[...] = base

    info = jnp.where(lane == 0, (i1 - ROUTER_LANE0).astype(F32), 0.0)
    info = jnp.where(lane == 1, (i2 - ROUTER_LANE0).astype(F32), info)
    info = jnp.where(lane == 2, w1, info)
    info = jnp.where(lane == 3, w2, info)
    info = jnp.where(lane == 4, rank1, info)
    info = jnp.where(lane == 5, rank2, info)
    info_ref[...] = info[:, :INFO_COLS]


def _merge(x2d, oa, yg, gate, woa, wor, wout, g2, wr, br, tm):
    wr_hi = wr.astype(BF16)
    wr_lo = (wr - wr_hi.astype(F32)).astype(BF16)
    m = x2d.shape[0]
    tile = lambda n: pl.BlockSpec((tm, n), lambda i: (i, 0))
    full = lambda a, b: pl.BlockSpec((a, b), lambda i: (0, 0))
    return pl.pallas_call(
        _merge_body,
        grid=(m // tm,),
        in_specs=[
            tile(D_MODEL), tile(ATT_Q), tile(RWKV_DIM), tile(GATE_COLS),
            full(ATT_Q, D_MODEL), full(RWKV_DIM, D_MODEL), full(D_MODEL, D_MODEL),
            full(1, D_MODEL), full(D_MODEL, LANES), full(D_MODEL, LANES), full(1, LANES),
        ],
        out_specs=[tile(D_MODEL), tile(D_MODEL), tile(INFO_COLS), full(1, LANES)],
        out_shape=[
            jax.ShapeDtypeStruct((m, D_MODEL), F32),
            jax.ShapeDtypeStruct((m, D_MODEL), F32),
            jax.ShapeDtypeStruct((m, INFO_COLS), F32),
            jax.ShapeDtypeStruct((1, LANES), F32),
        ],
        scratch_shapes=[pltpu.VMEM((1, LANES), F32)],
        compiler_params=_cparams("arbitrary"),
        name="merge",
    )(x2d, oa, yg, gate, woa, wor, wout, g2, wr_hi, wr_lo, br)


def _dispatch_body(max_trailing, pends_ref, dest_ref, xn_ref, xs_ref, zero_ref, sem, zsem):
    groups = xn_ref.shape[0]
    cap_blocks = xs_ref.shape[0] // SLOT_BLOCK

    @pl.when(pl.program_id(0) == 0)
    def _():
        zero_ref[...] = jnp.zeros_like(zero_ref)

        def zero_block(start):
            dst = xs_ref.at[pl.ds(pl.multiple_of(start, SLOT_BLOCK), SLOT_BLOCK)]
            return pltpu.make_async_copy(zero_ref, dst, zsem)

        ends = [pends_ref[e] for e in range(N_EXPERTS)]
        non_empty = [ends[e] > (ends[e - 1] if e else 0) for e in range(N_EXPERTS)]
        used_blocks = ends[-1] // SLOT_BLOCK
        trailing = [used_blocks + j < cap_blocks for j in range(max_trailing)]
        for e in range(N_EXPERTS):
            @pl.when(non_empty[e])
            def _():
                zero_block(ends[e] - SLOT_BLOCK).start()
        for j in range(max_trailing):
            @pl.when(trailing[j])
            def _():
                zero_block((used_blocks + j) * SLOT_BLOCK).start()
        for cond in non_empty + trailing:
            @pl.when(cond)
            def _():
                zero_block(0).wait()

    def row_copy(grp, sub, d):
        return pltpu.make_async_copy(xn_ref.at[grp, pl.ds(sub, 1)], xs_ref.at[pl.ds(d, 1)], sem)

    def start(grp, carry):
        for sub in range(SUBLANES):
            for k in range(TOP_K):
                row_copy(grp, sub, dest_ref[0, 0, grp * (SUBLANES * TOP_K) + sub * TOP_K + k]).start()
        return carry

    def wait(grp, carry):
        for _ in range(SUBLANES * TOP_K):
            row_copy(0, 0, 0).wait()
        return carry

    lax.fori_loop(0, groups, start, 0)
    lax.fori_loop(0, groups, wait, 0)


def _dispatch(pends, dest3, xn, cap):
    tm = dest3.shape[2] // TOP_K
    xn = xn.reshape(xn.shape[0] // SUBLANES, SUBLANES, D_MODEL)
    max_trailing = cap // SLOT_BLOCK - (dest3.shape[0] * dest3.shape[2]) // SLOT_BLOCK
    return pl.pallas_call(
        functools.partial(_dispatch_body, max_trailing),
        grid_spec=pltpu.PrefetchScalarGridSpec(
            num_scalar_prefetch=1,
            grid=(dest3.shape[0],),
            in_specs=[
                pl.BlockSpec((1, 1, dest3.shape[2]), lambda i, pe: (i, 0, 0), memory_space=pltpu.SMEM),
                pl.BlockSpec((tm // SUBLANES, SUBLANES, D_MODEL), lambda i, pe: (i, 0, 0)),
            ],
            out_specs=pl.BlockSpec(memory_space=pl.ANY),
            scratch_shapes=[pltpu.VMEM((SLOT_BLOCK, D_MODEL), F32), pltpu.SemaphoreType.DMA(()),
                            pltpu.SemaphoreType.DMA(())],
        ),
        out_shape=jax.ShapeDtypeStruct((cap, D_MODEL), F32),
        compiler_params=_cparams("arbitrary"),
        name="dispatch",
    )(pends, dest3, xn)


def _expert_body(be_ref, xs_ref, wg_ref, wu_ref, wd_ref, yb_ref):
    del be_ref
    xb = xs_ref[...].astype(BF16)
    gt = _dot(xb, wg_ref[...])
    up = _dot(xb, wu_ref[...])
    hmid = gt * _sigmoid(gt) * up
    yb_ref[...] = _dot(hmid.astype(BF16), wd_ref[...])


def _experts(block_expert, xs, wg, wu, wd):
    cap = xs.shape[0]
    nblk = cap // SLOT_BLOCK
    return pl.pallas_call(
        _expert_body,
        grid_spec=pltpu.PrefetchScalarGridSpec(
            num_scalar_prefetch=1,
            grid=(nblk,),
            in_specs=[
                pl.BlockSpec((SLOT_BLOCK, D_MODEL), lambda i, be: (i, 0)),
                pl.BlockSpec((None, D_MODEL, D_EXPERT), lambda i, be: (be[i], 0, 0)),
                pl.BlockSpec((None, D_MODEL, D_EXPERT), lambda i, be: (be[i], 0, 0)),
                pl.BlockSpec((None, D_EXPERT, D_MODEL), lambda i, be: (be[i], 0, 0)),
            ],
            out_specs=pl.BlockSpec((SLOT_BLOCK, D_MODEL), lambda i, be: (i, 0)),
        ),
        out_shape=jax.ShapeDtypeStruct((cap, D_MODEL), F32),
        compiler_params=_cparams("arbitrary"),
        name="experts",
    )(block_expert, xs, wg, wu, wd)


def _combine_body(dcur_ref, dnext_ref, h1_ref, info_ref, gf_ref, yb_ref, out_ref, buf_ref, sem):
    i = pl.program_id(0)
    tm = h1_ref.shape[0]
    groups = tm // SUBLANES

    def row_copy(d, slot, k, grp, sub):
        return pltpu.make_async_copy(yb_ref.at[pl.ds(d, 1)], buf_ref.at[slot, k, grp, pl.ds(sub, 1)], sem.at[slot])

    def issue(dref, slot):
        def start(grp, carry):
            for sub in range(SUBLANES):
                for k in range(TOP_K):
                    row_copy(dref[0, 0, grp * (SUBLANES * TOP_K) + sub * TOP_K + k], slot, k, grp, sub).start()
            return carry
        lax.fori_loop(0, groups, start, 0)

    def step(slot):
        @pl.when(i == 0)
        def _():
            issue(dcur_ref, slot)

        @pl.when(i + 1 < pl.num_programs(0))
        def _():
            issue(dnext_ref, 1 - slot)

        def wait(grp, carry):
            for _ in range(SUBLANES * TOP_K):
                row_copy(0, slot, 0, 0, 0).wait()
            return carry
        lax.fori_loop(0, groups, wait, 0)
        info = info_ref[...]
        y0 = buf_ref[slot, 0].reshape(tm, D_MODEL)
        y1 = buf_ref[slot, 1].reshape(tm, D_MODEL)
        y = y0 * info[:, 2:3] + y1 * info[:, 3:4]
        h2 = h1_ref[...] + y
        out_ref[...] = h2 * lax.rsqrt(jnp.mean(h2 * h2, axis=-1, keepdims=True) + NORM_EPS) * gf_ref[...]

    @pl.when(i % 2 == 0)
    def _():
        step(0)

    @pl.when(i % 2 == 1)
    def _():
        step(1)


def _combine(dest3, h1, info, gf, yb):
    m = h1.shape[0]
    nt = dest3.shape[0]
    tm = m // nt
    dspec = lambda f: pl.BlockSpec((1, 1, dest3.shape[2]), f, memory_space=pltpu.SMEM)
    return pl.pallas_call(
        _combine_body,
        grid=(nt,),
        in_specs=[
            dspec(lambda i: (i, 0, 0)),
            dspec(lambda i: (jnp.minimum(i + 1, nt - 1), 0, 0)),
            pl.BlockSpec((tm, D_MODEL), lambda i: (i, 0)),
            pl.BlockSpec((tm, INFO_COLS), lambda i: (i, 0)),
            pl.BlockSpec((1, D_MODEL), lambda i: (0, 0)),
            pl.BlockSpec(memory_space=pl.ANY),
        ],
        out_specs=pl.BlockSpec((tm, D_MODEL), lambda i: (i, 0)),
        out_shape=jax.ShapeDtypeStruct((m, D_MODEL), F32),
        scratch_shapes=[pltpu.VMEM((2, TOP_K, tm // SUBLANES, SUBLANES, D_MODEL), F32),
                        pltpu.SemaphoreType.DMA((2,))],
        compiler_params=_cparams("arbitrary"),
        name="combine",
    )(dest3, dest3, h1, info, gf, yb)


def _tile(m, pref):
    t = pref
    while m % t:
        t //= 2
    return t


def kernel(x, meta_tokens, norm_mix_g, w_in, shift_mu, w0, w_decay_up, a0, w_aaa_up, w_gate_up, k_k, k_a, r_k,
           ln_x_g, ln_x_b, attn_sinks, w_o_attn, w_o_rwkv, w_out, norm_ffn_g, w_grp, b_grp, w_exp, b_exp,
           e_gate, e_up, e_down, norm_final_g):
    assert norm_mix_g.shape[0] == 1, "single-layer trunk"
    b, s, d = x.shape
    assert d == D_MODEL and s % BLOCK == 0
    m = b * s
    row = lambda t: t.reshape(1, -1).astype(F32)
    w_in_b = w_in[0].astype(BF16)
    g_mix = row(norm_mix_g[0])
    rp = dict(mu=row(shift_mu[0]),
              rows=jnp.concatenate([row(t[0]) for t in (w0, a0, k_k, k_a, r_k, ln_x_g, ln_x_b)]
                                   + [jnp.zeros((1, RWKV_DIM), F32)], axis=0),
              lora=jnp.concatenate([w_decay_up[0], w_aaa_up[0], w_gate_up[0]], axis=0).astype(BF16))

    x0 = jnp.concatenate([jnp.zeros((BLOCK - N_META, d), F32), meta_tokens.astype(F32)], axis=0)
    qkv0, rw0, _ = _proj(x0, g_mix, w_in_b, BLOCK)
    kmeta = qkv0[BLOCK - N_META:, ATT_Q:ATT_Q + ATT_KV]
    vmeta = qkv0[BLOCK - N_META:, ATT_Q + ATT_KV:]
    zero_state = jnp.zeros((RWKV_DIM // LANES, LANES, LANES), F32)
    _, s_meta = _rwkv(rw0[None], jnp.zeros((1, RWKV_COLS), F32), zero_state, rp)

    x2d = x.reshape(m, d)
    qkv, rw, gate = _proj(x2d, g_mix, w_in_b, _tile(m, ROW_TILE))
    o_att = _attention(qkv.reshape(b, s, ATT_COLS), kmeta, vmeta, attn_sinks[0].astype(F32))
    yg, _ = _rwkv(rw.reshape(b, s, RWKV_COLS), rw0[BLOCK - 1:], s_meta, rp)

    wr = jnp.zeros((d, LANES), F32).at[:, :N_GROUPS].set(w_grp[0]).at[:, N_GROUPS:N_GROUPS + N_EXPERTS].set(w_exp[0])
    br = jnp.zeros((1, LANES), F32).at[0, :N_GROUPS].set(b_grp[0]).at[0, N_GROUPS:N_GROUPS + N_EXPERTS].set(b_exp[0])
    tm = _tile(m, DMA_TILE)
    h1, xn2, info, cnt = _merge(x2d, o_att.reshape(m, ATT_Q), yg.reshape(m, RWKV_DIM), gate,
                                w_o_attn[0].astype(BF16), w_o_rwkv[0].astype(BF16), w_out[0].astype(BF16),
                                row(norm_ffn_g[0]), wr, br, _tile(m, ROW_TILE))

    counts = cnt[0, N_GROUPS:N_GROUPS + N_EXPERTS].astype(jnp.int32)
    padded = ((counts + SLOT_BLOCK - 1) // SLOT_BLOCK) * SLOT_BLOCK
    pends = jnp.cumsum(padded)
    pstarts = pends - padded
    eid = info[:, 0:2].astype(jnp.int32)
    dest = pstarts[eid] + info[:, 4:6].astype(jnp.int32)
    nblk = -(-(m * TOP_K + N_EXPERTS * (SLOT_BLOCK - 1)) // SLOT_BLOCK)
    cap = nblk * SLOT_BLOCK
    block_start = jnp.arange(nblk, dtype=jnp.int32) * SLOT_BLOCK
    block_expert = jnp.minimum(jnp.sum(block_start[:, None] >= pends[None, :], axis=1), N_EXPERTS - 1).astype(jnp.int32)
    dest3 = dest.reshape(m // tm, 1, 2 * tm)

    xs = _dispatch(pends.astype(jnp.int32), dest3, xn2, cap)
    yb = _experts(block_expert, xs, e_gate[0].astype(BF16), e_up[0].astype(BF16), e_down[0].astype(BF16))
    out = _combine(dest3, h1, info, row(norm_final_g), yb)
    return out.reshape(b, s, d)
```
